```python
import jax, jax.numpy as jnp
from jax import lax
import numpy as np

D_MODEL = 2048
BATCH = 4
SEQ = 2048
DEPTH = 2
DEC_BATCH = 32
DEC_SEQ = 4
PAST_LEN = 8192
PAGE_SIZE = 128

H_A = 8
DK_A = 128
DV_A = 128
QK_A = H_A * DK_A
V_A = H_A * DV_A
CONV_K = 4
CONV_CH = 2 * QK_A + V_A
GDN_CHUNK = 64
DIL_GROUPS = ((128, 1), (512, 4), (2048, 16))
N_GROUPS = len(DIL_GROUPS)
H_G = 4
HD_B = 128
QKV_B = N_GROUPS * H_G * HD_B
OUT_B = H_G * HD_B
Q_BLOCK = 128
POOL_WINDOWS = (2, 4, 8, 16)
C_POOL = 1024
CG = C_POOL // len(POOL_WINDOWS)
POOL_HIST = max(POOL_WINDOWS) - 1
D_FF = ((8 * D_MODEL + 3 * 256 - 1) // (3 * 256)) * 256
EPS = 1e-6
L2_EPS = 1e-6
IN_SPLITS = (QK_A, QK_A, V_A, V_A, H_A, H_A, QKV_B, QKV_B, QKV_B, C_POOL, D_MODEL, D_MODEL, D_MODEL)
N_IN = sum(IN_SPLITS)

kernel_name = 'hybrid_gdn_dilated_pool_decoder_step'


def _rms_norm(x, gain):
    xf = x.astype(jnp.float32)
    y = xf * lax.rsqrt(jnp.mean(xf * xf, axis=-1, keepdims=True) + EPS)
    return (y * gain.astype(jnp.float32)).astype(x.dtype)


def _split(x, sizes):
    cuts = [int(c) for c in np.cumsum(sizes)[:-1]]
    return jnp.split(x, cuts, axis=-1)


def _l2norm(x):
    return x * lax.rsqrt(jnp.sum(x * x, axis=-1, keepdims=True) + L2_EPS)


def _causal_conv(hist, u, w):
    T = u.shape[1]
    xp = jnp.concatenate([hist.astype(u.dtype), u], axis=1)
    out = xp[:, 0:T] * w[0]
    for j in range(1, CONV_K):
        out = out + xp[:, j:j + T] * w[j]
    return out, xp[:, -(CONV_K - 1):]


def _gdn_chunked(q, k, v, g, beta, s0, chunk):
    Bn, T, H, DK = q.shape
    DV = v.shape[-1]
    N = T // chunk

    def blocks(t):
        return t.reshape(Bn, N, chunk, H, t.shape[-1]).transpose(1, 0, 3, 2, 4)

    qc, kc, vc = blocks(q), blocks(k), blocks(v)
    gc = jnp.cumsum(g.reshape(Bn, N, chunk, H).transpose(1, 0, 3, 2), axis=-1)
    bc = beta.reshape(Bn, N, chunk, H).transpose(1, 0, 3, 2)
    incl = jnp.tril(jnp.ones((chunk, chunk), dtype=bool))
    strict = jnp.tril(jnp.ones((chunk, chunk), dtype=bool), k=-1)
    diff = gc[..., :, None] - gc[..., None, :]
    decay = jnp.where(incl, jnp.exp(jnp.where(incl, diff, 0.0)), 0.0)
    kb = kc * bc[..., None]
    lower = jnp.where(strict, jnp.einsum('nbhid,nbhjd->nbhij', kb, kc) * decay, 0.0)
    eye = jnp.eye(chunk, dtype=jnp.float32)
    tinv = lax.linalg.triangular_solve(eye + lower, jnp.broadcast_to(eye, lower.shape), left_side=True, lower=True)
    u = tinv @ (vc * bc[..., None])
    w = tinv @ (kb * jnp.exp(gc)[..., None])
    a_intra = jnp.where(incl, jnp.einsum('nbhid,nbhjd->nbhij', qc, kc) * decay, 0.0)

    def step(s, xs):
        qi, ki, ui, wi, gi, ai = xs
        v_new = ui - wi @ s
        o = (qi * jnp.exp(gi)[..., None]) @ s + ai @ v_new
        g_last = gi[..., -1]
        s = s * jnp.exp(g_last)[..., None, None] + jnp.einsum('bhcd,bhce->bhde', ki * jnp.exp(g_last[..., None] - gi)[..., None], v_new)
        return s, o

    s_final, o = lax.scan(step, s0, (qc, kc, u, w, gc, a_intra))
    o = o.transpose(1, 0, 3, 2, 4).reshape(Bn, T, H, DV)
    return o, s_final


def _dilated_group(q, k, v, q_idx, dil, n_back):
    dist = jnp.arange(n_back + 1) * dil
    idx = q_idx[:, None] - dist[None, :]
    valid = idx >= 0
    idx = jnp.maximum(idx, 0)
    kg = k[:, idx].astype(jnp.float32)
    vg = v[:, idx].astype(jnp.float32)
    s = jnp.einsum('bqhd,bqjhd->bqhj', q.astype(jnp.float32), kg) * (HD_B ** -0.5)
    s = jnp.where(valid[None, :, None, :], s, -jnp.inf)
    m = jnp.max(s, axis=-1)
    p = jnp.exp(s - m[..., None])
    return jnp.einsum('bqhj,bqjhd->bqhd', p, vg), m, jnp.sum(p, axis=-1)


def _dilated_mixture(qs, ks, vs, q_idxs):
    parts = [_dilated_group(q, k, v, qi, dil, win // dil)
             for q, k, v, qi, (win, dil) in zip(qs, ks, vs, q_idxs, DIL_GROUPS)]
    m_all = jnp.stack([pt[1] for pt in parts])
    e = jnp.exp(m_all - jnp.max(m_all, axis=0))
    num = sum(e[i][..., None] * parts[i][0] for i in range(N_GROUPS))
    den = sum(e[i] * parts[i][2] for i in range(N_GROUPS))
    return num / den[..., None]


def _dilated_prompt(q, k, v):
    Bn, S = q.shape[:2]
    nb = S // Q_BLOCK
    qb = q.reshape(Bn, nb, Q_BLOCK, N_GROUPS, H_G, HD_B).transpose(1, 0, 2, 3, 4, 5)
    k_g = [k[:, :, gi] for gi in range(N_GROUPS)]
    v_g = [v[:, :, gi] for gi in range(N_GROUPS)]

    def blk(args):
        i, qi = args
        q_idx = i * Q_BLOCK + jnp.arange(Q_BLOCK)
        return _dilated_mixture([qi[:, :, gi] for gi in range(N_GROUPS)], k_g, v_g, [q_idx] * N_GROUPS)

    o = lax.map(blk, (jnp.arange(nb), qb))
    return o.transpose(1, 0, 2, 3, 4).reshape(Bn, S, H_G, HD_B)


def _multiscale_pool(hist, u, pos):
    T = u.shape[1]
    xp = jnp.concatenate([hist.astype(u.dtype), u], axis=1)
    xf = xp.astype(jnp.float32)
    csum = jnp.concatenate([jnp.zeros_like(xf[:, :1]), jnp.cumsum(xf, axis=1)], axis=1)
    end = csum[:, POOL_HIST + 1:POOL_HIST + 1 + T]
    means = []
    for gi, win in enumerate(POOL_WINDOWS):
        ch = slice(gi * CG, (gi + 1) * CG)
        start = csum[:, POOL_HIST + 1 - win:POOL_HIST + 1 - win + T, ch]
        cnt = jnp.minimum(win, pos + 1).astype(jnp.float32)[None, :, None]
        means.append((end[..., ch] - start) / cnt)
    mixed = jnp.concatenate(means, axis=-1) - xf[:, POOL_HIST:]
    return mixed.astype(u.dtype), xp[:, -POOL_HIST:]


def _layer(x, lp, hist, prompt):
    (w_in, conv_w, a_log, dt_bias, gdn_gain, w_pool, pool_scale,
     w_br_a, w_br_b, w_br_c, w_out, w_gu, w_down,
     g_pre_mix, g_post_mix, g_pre_ffn, g_post_ffn) = lp
    conv_hist, s0, pool_hist, win_hists = hist
    Bn, T, _ = x.shape
    dt = x.dtype
    h = _rms_norm(x, g_pre_mix)
    (qa, ka, va, za, ba, aa, qb, kb, vb, uc, ga, gb, gc) = _split(h @ w_in, IN_SPLITS)

    qkv, conv_new = _causal_conv(conv_hist, jnp.concatenate([qa, ka, va], axis=-1), conv_w)
    qkv = jax.nn.silu(qkv).astype(jnp.float32)
    q, k, v = _split(qkv, (QK_A, QK_A, V_A))
    q = _l2norm(q.reshape(Bn, T, H_A, DK_A)) * (DK_A ** -0.5)
    k = _l2norm(k.reshape(Bn, T, H_A, DK_A))
    v = v.reshape(Bn, T, H_A, DV_A)
    beta = jax.nn.sigmoid(ba.astype(jnp.float32))
    g = -jnp.exp(a_log.astype(jnp.float32)) * jax.nn.softplus(aa.astype(jnp.float32) + dt_bias.astype(jnp.float32))
    chunk = GDN_CHUNK if prompt else T
    o, s_new = _gdn_chunked(q, k, v, g, beta, s0.astype(jnp.float32), chunk)
    o = o * lax.rsqrt(jnp.mean(o * o, axis=-1, keepdims=True) + EPS) * gdn_gain.astype(jnp.float32)
    o = o * jax.nn.silu(za.astype(jnp.float32).reshape(Bn, T, H_A, DV_A))
    out_a = o.reshape(Bn, T, V_A).astype(dt)

    qb = qb.reshape(Bn, T, N_GROUPS, H_G, HD_B)
    kb = kb.reshape(Bn, T, N_GROUPS, H_G, HD_B)
    vb = vb.reshape(Bn, T, N_GROUPS, H_G, HD_B)
    kv_new = [jnp.stack([kb[:, :, gi], vb[:, :, gi]], axis=2) for gi in range(N_GROUPS)]
    if prompt:
        ob = _dilated_prompt(qb, kb, vb)
        kv_all = kv_new
    else:
        kv_all = [jnp.concatenate([win_hists[gi].astype(dt), kv_new[gi]], axis=1) for gi in range(N_GROUPS)]
        q_idxs = [kv.shape[1] - T + jnp.arange(T) for kv in kv_all]
        ob = _dilated_mixture([qb[:, :, gi] for gi in range(N_GROUPS)],
                              [kv[:, :, 0] for kv in kv_all], [kv[:, :, 1] for kv in kv_all], q_idxs)
    win_new = [kv_all[gi][:, -win:] for gi, (win, _) in enumerate(DIL_GROUPS)]
    out_b = ob.reshape(Bn, T, OUT_B).astype(dt)

    pos = jnp.arange(T) + (0 if prompt else PAST_LEN)
    pooled, pool_new = _multiscale_pool(pool_hist, uc, pos)
    oc = jnp.einsum('btgc,gcd->btgd', pooled.reshape(Bn, T, len(POOL_WINDOWS), CG), w_pool).reshape(Bn, T, C_POOL) * pool_scale

    merged = (jax.nn.sigmoid(ga) * (out_a @ w_br_a)
              + jax.nn.sigmoid(gb) * (out_b @ w_br_b)
              + jax.nn.sigmoid(gc) * (oc @ w_br_c))
    x = x + _rms_norm(merged @ w_out, g_post_mix)

    gate, up = _split(_rms_norm(x, g_pre_ffn) @ w_gu, (D_FF, D_FF))
    x = x + _rms_norm((jax.nn.silu(gate) * up) @ w_down, g_post_ffn)
    return x, (win_new[0], win_new[1], win_new[2], s_new.astype(dt), conv_new, pool_new)


def setup_inputs(seed: int = 0) -> dict:
    key = jax.random.key(seed)
    ks = jax.random.split(key, 32)
    f32 = jnp.float32

    def nrm(k, shape, s=1.0):
        return jax.random.normal(k, shape, f32) * s

    wb = [min(win, PAST_LEN) for win, _ in DIL_GROUPS]
    return {
        'x_prompt': nrm(ks[0], (BATCH, SEQ, D_MODEL)),
        'x_sample': nrm(ks[1], (DEC_BATCH, DEC_SEQ, D_MODEL)),
        'cache_win1': nrm(ks[2], (DEPTH, DEC_BATCH, wb[0], 2, H_G, HD_B)),
        'cache_win2': nrm(ks[3], (DEPTH, DEC_BATCH, wb[1], 2, H_G, HD_B)),
        'cache_win3': nrm(ks[4], (DEPTH, DEC_BATCH, wb[2], 2, H_G, HD_B)),
        'state_gdn': nrm(ks[5], (DEPTH, DEC_BATCH, H_A, DK_A, DV_A), 0.1),
        'state_conv': nrm(ks[6], (DEPTH, DEC_BATCH, CONV_K - 1, CONV_CH)),
        'state_pool': nrm(ks[7], (DEPTH, DEC_BATCH, POOL_HIST, C_POOL)),
        'w_in': nrm(ks[8], (DEPTH, D_MODEL, N_IN), D_MODEL ** -0.5),
        'conv_w': nrm(ks[9], (DEPTH, CONV_K, CONV_CH), CONV_K ** -0.5),
        'a_log': jnp.log(jax.random.uniform(ks[10], (DEPTH, H_A), f32, 1.0, 16.0)),
        'dt_bias': jnp.log(jnp.expm1(jax.random.uniform(ks[11], (DEPTH, H_A), f32, 1e-3, 0.1))),
        'gdn_gain': 1.0 + nrm(ks[12], (DEPTH, DV_A), 0.02),
        'w_pool': nrm(ks[13], (DEPTH, len(POOL_WINDOWS), CG, CG), CG ** -0.5),
        'pool_scale': 1.0 + nrm(ks[14], (DEPTH, C_POOL), 0.02),
        'w_br_a': nrm(ks[15], (DEPTH, V_A, D_MODEL), V_A ** -0.5),
        'w_br_b': nrm(ks[16], (DEPTH, OUT_B, D_MODEL), OUT_B ** -0.5),
        'w_br_c': nrm(ks[17], (DEPTH, C_POOL, D_MODEL), C_POOL ** -0.5),
        'w_out': nrm(ks[18], (DEPTH, D_MODEL, D_MODEL), D_MODEL ** -0.5),
        'w_gu': nrm(ks[19], (DEPTH, D_MODEL, 2 * D_FF), D_MODEL ** -0.5),
        'w_down': nrm(ks[20], (DEPTH, D_FF, D_MODEL), D_FF ** -0.5),
        'g_pre_mix': 1.0 + nrm(ks[21], (DEPTH, D_MODEL), 0.02),
        'g_post_mix': 1.0 + nrm(ks[22], (DEPTH, D_MODEL), 0.02),
        'g_pre_ffn': 1.0 + nrm(ks[23], (DEPTH, D_MODEL), 0.02),
        'g_post_ffn': 1.0 + nrm(ks[24], (DEPTH, D_MODEL), 0.02),
    }


def reference(x_prompt, x_sample, cache_win1, cache_win2, cache_win3, state_gdn, state_conv, state_pool,
              w_in, conv_w, a_log, dt_bias, gdn_gain, w_pool, pool_scale, w_br_a, w_br_b, w_br_c,
              w_out, w_gu, w_down, g_pre_mix, g_post_mix, g_pre_ffn, g_post_ffn):
    yp, ys = x_prompt, x_sample
    bp = x_prompt.shape[0]
    dt = x_prompt.dtype
    new_p, new_s = [], []
    for l in range(DEPTH):
        lp = (w_in[l], conv_w[l], a_log[l], dt_bias[l], gdn_gain[l], w_pool[l], pool_scale[l],
              w_br_a[l], w_br_b[l], w_br_c[l], w_out[l], w_gu[l], w_down[l],
              g_pre_mix[l], g_post_mix[l], g_pre_ffn[l], g_post_ffn[l])
        hist_p = (jnp.zeros((bp, CONV_K - 1, CONV_CH), dt),
                  jnp.zeros((bp, H_A, DK_A, DV_A), jnp.float32),
                  jnp.zeros((bp, POOL_HIST, C_POOL), dt),
                  None)
        yp, st_p = _layer(yp, lp, hist_p, True)
        hist_s = (state_conv[l], state_gdn[l], state_pool[l], (cache_win1[l], cache_win2[l], cache_win3[l]))
        ys, st_s = _layer(ys, lp, hist_s, False)
        new_p.append(st_p)
        new_s.append(st_s)
    pw1, pw2, pw3, pgdn, pconv, ppool = [jnp.stack([st[i] for st in new_p], axis=0) for i in range(6)]
    sw1, sw2, sw3, sgdn, sconv, spool = [jnp.stack([st[i] for st in new_s], axis=0) for i in range(6)]
    return (yp, ys, pw1, pw2, pw3, pgdn, pconv, ppool, sw1, sw2, sw3, sgdn, sconv, spool)
```

```python
import functools

import jax
import jax.numpy as jnp
from jax import lax
from jax.experimental import pallas as pl
from jax.experimental.pallas import tpu as pltpu

F32 = jnp.float32
BF16 = jnp.bfloat16

H_A = 8
DK_A = 128
CONV_K = 4
DIL_GROUPS = ((128, 1), (512, 4), (2048, 16))
N_GROUPS = len(DIL_GROUPS)
H_G = 4
HD_B = 128
N_BACK = 128
POOL_WINDOWS = (2, 4, 8, 16)
CG = 256
C_POOL = CG * len(POOL_WINDOWS)
POOL_HIST = max(POOL_WINDOWS) - 1
SAMPLE_POS0 = 8192
EPS = 1e-6
L2_EPS = 1e-6

LANE = 128
SUBLANE = 8
VMEM_LIMIT = 48 * 1024 * 1024

QK_A = H_A * DK_A
QKV_B = N_GROUPS * H_G * HD_B
OUT_B = H_G * HD_B
GDN_CHUNK = 64
GDN_TILE = 256


def _proj_layout(d_model):
    off = {}
    c = 0
    for name, width in (("qb", QKV_B), ("kb", QKV_B), ("vb", QKV_B),
                        ("ba", H_A), ("aa", H_A), ("pad0", 512 - 2 * H_A),
                        ("uc", C_POOL),
                        ("qa", QK_A), ("ka", QK_A), ("va", QK_A), ("za", QK_A),
                        ("ga", d_model), ("gb", d_model), ("gc", d_model)):
        off[name] = c
        c += width
    total = -(-c // 512) * 512
    return off, total


def _prep_w_in(w, d_model):
    sizes = (QK_A, QK_A, QK_A, QK_A, H_A, H_A, QKV_B, QKV_B, QKV_B, C_POOL, d_model, d_model, d_model)
    cuts = [0]
    for s in sizes:
        cuts.append(cuts[-1] + s)
    qa, ka, va, za, ba, aa, qb, kb, vb, uc, ga, gb, gc = [w[:, cuts[i]:cuts[i + 1]] for i in range(len(sizes))]
    _, total = _proj_layout(d_model)
    pad0 = jnp.zeros((w.shape[0], 512 - 2 * H_A), w.dtype)
    parts = [qb, kb, vb, ba, aa, pad0, uc, qa, ka, va, za, ga, gb, gc]
    used = sum(p.shape[1] for p in parts)
    if total > used:
        parts.append(jnp.zeros((w.shape[0], total - used), w.dtype))
    return jnp.concatenate(parts, axis=1).astype(BF16)


def _cparams(sem):
    return pltpu.CompilerParams(dimension_semantics=sem, vmem_limit_bytes=VMEM_LIMIT)


def _dot(a, b):
    return jnp.dot(a.astype(BF16), b.astype(BF16), preferred_element_type=F32)


def _dot_nt(a, b):
    return lax.dot_general(a.astype(BF16), b.astype(BF16), (((1,), (1,)), ((), ())),
                           preferred_element_type=F32)


def _dot_tn(a, b):
    return lax.dot_general(a.astype(BF16), b.astype(BF16), (((0,), (0,)), ((), ())),
                           preferred_element_type=F32)


def _dot3(a, b):
    a_hi = a.astype(BF16)
    b_hi = b.astype(BF16)
    a_lo = (a - a_hi.astype(F32)).astype(BF16)
    b_lo = (b - b_hi.astype(F32)).astype(BF16)
    out = jnp.dot(a_hi, b_hi, preferred_element_type=F32)
    out = out + jnp.dot(a_hi, b_lo, preferred_element_type=F32)
    return out + jnp.dot(a_lo, b_hi, preferred_element_type=F32)


def _rms_scale(y):
    return lax.rsqrt(jnp.mean(y * y, axis=-1, keepdims=True) + EPS)


def _silu(x):
    return x * jax.nn.sigmoid(x)


def _norm_mm_body(x_ref, g_ref, w_ref, o_ref, h_ref):
    @pl.when(pl.program_id(1) == 0)
    def _():
        xf = x_ref[...]
        h_ref[...] = (xf * _rms_scale(xf) * g_ref[...]).astype(BF16)

    o_ref[...] = jnp.dot(h_ref[...], w_ref[...], preferred_element_type=F32)


def _norm_matmul(x, gain, w, tm, tn):
    m, d = x.shape
    n = w.shape[1]
    return pl.pallas_call(
        _norm_mm_body,
        grid=(m // tm, n // tn),
        in_specs=[pl.BlockSpec((tm, d), lambda i, j: (i, 0)),
                  pl.BlockSpec((1, d), lambda i, j: (0, 0)),
                  pl.BlockSpec((d, tn), lambda i, j: (0, j))],
        out_specs=pl.BlockSpec((tm, tn), lambda i, j: (i, j)),
        out_shape=jax.ShapeDtypeStruct((m, n), F32),
        scratch_shapes=[pltpu.VMEM((tm, d), BF16)],
        compiler_params=_cparams(("parallel", "arbitrary")),
        name="in_proj",
    )(x, gain.reshape(1, d), w)


def _ffn_up_body(x_ref, g_ref, wg_ref, wu_ref, o_ref, h_ref):
    @pl.when(pl.program_id(1) == 0)
    def _():
        xf = x_ref[...]
        h_ref[...] = (xf * _rms_scale(xf) * g_ref[...]).astype(BF16)

    h = h_ref[...]
    gate = jnp.dot(h, wg_ref[...], preferred_element_type=F32)
    up = jnp.dot(h, wu_ref[...], preferred_element_type=F32)
    o_ref[...] = (_silu(gate) * up).astype(BF16)


def _ffn_up(x, gain, w_gu, tm, tn):
    m, d = x.shape
    d_ff = w_gu.shape[1] // 2
    nj = d_ff // tn
    return pl.pallas_call(
        _ffn_up_body,
        grid=(m // tm, nj),
        in_specs=[pl.BlockSpec((tm, d), lambda i, j: (i, 0)),
                  pl.BlockSpec((1, d), lambda i, j: (0, 0)),
                  pl.BlockSpec((d, tn), lambda i, j: (0, j)),
                  pl.BlockSpec((d, tn), lambda i, j: (0, j + nj))],
        out_specs=pl.BlockSpec((tm, tn), lambda i, j: (i, j)),
        out_shape=jax.ShapeDtypeStruct((m, d_ff), BF16),
        scratch_shapes=[pltpu.VMEM((tm, d), BF16)],
        compiler_params=_cparams(("parallel", "arbitrary")),
        name="ffn_up",
    )(x, gain.reshape(1, d), w_gu, w_gu)


def _mm_norm_res_body(a_ref, w_ref, res_ref, g_ref, o_ref, acc_ref, *, nk):
    k = pl.program_id(1)

    @pl.when(k == 0)
    def _():
        acc_ref[...] = jnp.zeros_like(acc_ref)

    acc_ref[...] += jnp.dot(a_ref[...], w_ref[...], preferred_element_type=F32)

    @pl.when(k == nk - 1)
    def _():
        y = acc_ref[...]
        o_ref[...] = res_ref[...] + y * _rms_scale(y) * g_ref[...]


def _mm_norm_res(a, w, res, gain, tm, tk):
    m, kdim = a.shape
    d = w.shape[1]
    nk = kdim // tk
    return pl.pallas_call(
        functools.partial(_mm_norm_res_body, nk=nk),
        grid=(m // tm, nk),
        in_specs=[pl.BlockSpec((tm, tk), lambda i, k: (i, k)),
                  pl.BlockSpec((tk, d), lambda i, k: (k, 0)),
                  pl.BlockSpec((tm, d), lambda i, k: (i, 0)),
                  pl.BlockSpec((1, d), lambda i, k: (0, 0))],
        out_specs=pl.BlockSpec((tm, d), lambda i, k: (i, 0)),
        out_shape=jax.ShapeDtypeStruct((m, d), F32),
        scratch_shapes=[pltpu.VMEM((tm, d), F32)],
        compiler_params=_cparams(("parallel", "arbitrary")),
        name="mm_norm_res",
    )(a, w, res, gain.reshape(1, d))


def _merge_body(a_ref, b_ref, c_ref, wa_ref, wb_ref, wc_ref, ga_ref, gb_ref, gc_ref, o_ref):
    m = jax.nn.sigmoid(ga_ref[...]) * jnp.dot(a_ref[...], wa_ref[...], preferred_element_type=F32)
    m = m + jax.nn.sigmoid(gb_ref[...]) * jnp.dot(b_ref[...], wb_ref[...], preferred_element_type=F32)
    m = m + jax.nn.sigmoid(gc_ref[...]) * jnp.dot(c_ref[...], wc_ref[...], preferred_element_type=F32)
    o_ref[...] = m.astype(BF16)


def _merge(out_a, out_b, out_c, w_a, w_b, w_c, proj2d, off, tm, tn):
    m = out_a.shape[0]
    d = w_a.shape[1]
    ja, jb, jc = off["ga"] // tn, off["gb"] // tn, off["gc"] // tn
    return pl.pallas_call(
        _merge_body,
        grid=(m // tm, d // tn),
        in_specs=[pl.BlockSpec((tm, out_a.shape[1]), lambda i, j: (i, 0)),
                  pl.BlockSpec((tm, out_b.shape[1]), lambda i, j: (i, 0)),
                  pl.BlockSpec((tm, out_c.shape[1]), lambda i, j: (i, 0)),
                  pl.BlockSpec((w_a.shape[0], tn), lambda i, j: (0, j)),
                  pl.BlockSpec((w_b.shape[0], tn), lambda i, j: (0, j)),
                  pl.BlockSpec((w_c.shape[0], tn), lambda i, j: (0, j)),
                  pl.BlockSpec((tm, tn), lambda i, j: (i, j + ja)),
                  pl.BlockSpec((tm, tn), lambda i, j: (i, j + jb)),
                  pl.BlockSpec((tm, tn), lambda i, j: (i, j + jc))],
        out_specs=pl.BlockSpec((tm, tn), lambda i, j: (i, j)),
        out_shape=jax.ShapeDtypeStruct((m, d), BF16),
        compiler_params=_cparams(("parallel", "arbitrary")),
        name="merge",
    )(out_a, out_b, out_c, w_a, w_b, w_c, proj2d, proj2d, proj2d)


def _unit_lower_inverse(lower, c):
    row = lax.broadcasted_iota(jnp.int32, (c, c), 0)
    col = lax.broadcasted_iota(jnp.int32, (c, c), 1)

    def same_block(shift):
        return (row >> shift) == (col >> shift)

    eye = jnp.where(row == col, 1.0, 0.0).astype(F32)
    d1 = jnp.where(same_block(3), lower, 0.0)
    d2 = _dot3(d1, d1)
    d4 = _dot3(d2, d2)
    x = eye - d1
    x = x + _dot3(d2, x)
    x = x + _dot3(d4, x)
    shift = 3
    while (1 << shift) < c:
        off_diag = jnp.where(jnp.logical_and(same_block(shift + 1), jnp.logical_not(same_block(shift))),
                             lower, 0.0)
        x = x - _dot3(_dot3(x, off_diag), x)
        shift += 1
    return x


def _gdn_body(alog_ref, dtb_ref, q_ref, k_ref, v_ref, z_ref, ba_ref, cwq_ref, cwk_ref, cwv_ref,
              hq_ref, hk_ref, hv_ref, s0_ref, gain_ref, o_ref, sout_ref,
              s_scr, ext_scr, z_scr, ba_scr, *, tt, tr, t_valid, n_tiles):
    c = GDN_CHUNK
    h = pl.program_id(1)
    it = pl.program_id(2)

    @pl.when(it == 0)
    def _():
        s_scr[...] = s0_ref[0, 0]
        ext_scr[0, 0:SUBLANE, :] = hq_ref[0]
        ext_scr[1, 0:SUBLANE, :] = hk_ref[0]
        ext_scr[2, 0:SUBLANE, :] = hv_ref[0]

    if tr < tt:
        zeros = jnp.zeros((tt, LANE), F32)
        for i in range(3):
            ext_scr[i, SUBLANE:, :] = zeros
        z_scr[...] = zeros
        ba_scr[...] = zeros
    ext_scr[0, SUBLANE:SUBLANE + tr, :] = q_ref[0]
    ext_scr[1, SUBLANE:SUBLANE + tr, :] = k_ref[0]
    ext_scr[2, SUBLANE:SUBLANE + tr, :] = v_ref[0]
    z_scr[0:tr, :] = z_ref[0]
    ba_scr[0:tr, :] = ba_ref[0]

    def conv_act(i, cw_ref):
        base = SUBLANE - (CONV_K - 1)
        y = ext_scr[i, base:base + tt, :] * cw_ref[0:1, :]
        for j in range(1, CONV_K):
            y = y + ext_scr[i, base + j:base + j + tt, :] * cw_ref[j:j + 1, :]
        return _silu(y)

    q = conv_act(0, cwq_ref)
    k = conv_act(1, cwk_ref)
    v = conv_act(2, cwv_ref)
    q = q * lax.rsqrt(jnp.sum(q * q, axis=-1, keepdims=True) + L2_EPS) * (DK_A ** -0.5)
    k = k * lax.rsqrt(jnp.sum(k * k, axis=-1, keepdims=True) + L2_EPS)

    if n_tiles > 1:
        for i in range(3):
            ext_scr[i, 0:SUBLANE, :] = ext_scr[i, tt:tt + SUBLANE, :]

    ba = ba_scr[...]
    lane = lax.broadcasted_iota(jnp.int32, (tt, LANE), 1)
    b_raw = jnp.sum(jnp.where(lane == h, ba, 0.0), axis=1, keepdims=True)
    a_raw = jnp.sum(jnp.where(lane == h + H_A, ba, 0.0), axis=1, keepdims=True)
    beta = jax.nn.sigmoid(b_raw)
    a_log = jnp.full((1, 1), alog_ref[h], F32)
    xs = a_raw + dtb_ref[h]
    softplus = jnp.maximum(xs, 0.0) + jnp.log1p(jnp.exp(-jnp.abs(xs)))
    g = -jnp.exp(a_log) * softplus

    if t_valid < tt:
        valid = lax.broadcasted_iota(jnp.int32, (tt, 1), 0) < t_valid
        q = jnp.where(valid, q, 0.0)
        k = jnp.where(valid, k, 0.0)
        v = jnp.where(valid, v, 0.0)
        beta = jnp.where(valid, beta, 0.0)
        g = jnp.where(valid, g, 0.0)

    row = lax.broadcasted_iota(jnp.int32, (c, c), 0)
    col = lax.broadcasted_iota(jnp.int32, (c, c), 1)
    incl = row >= col
    strict = row > col
    eye = row == col
    tri = jnp.where(incl, 1.0, 0.0).astype(F32)

    pre = []
    for ci in range(tt // c):
        sl = slice(ci * c, (ci + 1) * c)
        qc, kc, vc, bc, gcol = q[sl], k[sl], v[sl], beta[sl], g[sl]
        gmat = jnp.broadcast_to(gcol, (c, c))
        gc_cb = jnp.dot(tri, gmat, preferred_element_type=F32, precision=lax.Precision.HIGHEST)
        gc_rb = jnp.sum(jnp.where(eye, gc_cb, 0.0), axis=0, keepdims=True)
        decay = jnp.where(incl, jnp.exp(jnp.where(incl, gc_cb - gc_rb, 0.0)), 0.0)
        kb = kc * bc
        lower = jnp.where(strict, _dot_nt(kb, kc) * decay, 0.0)
        tinv = _unit_lower_inverse(lower, c)
        gc_col = gc_cb[:, 0:1]
        egc = jnp.exp(gc_col)
        u = _dot3(tinv, vc * bc)
        w = _dot3(tinv, kb * egc)
        a_intra = jnp.where(incl, _dot_nt(qc, kc) * decay, 0.0)
        g_last = gc_cb[c - 1:c, 0:1]
        kd = kc * jnp.exp(g_last - gc_col)
        pre.append((u, w, a_intra, kd, qc * egc, jnp.exp(g_last)))

    s = s_scr[...]
    outs = []
    for (u, w, a_intra, kd, qe, eg_last) in pre:
        v_new = u - _dot(w, s)
        outs.append(_dot(qe, s) + _dot(a_intra, v_new))
        s = s * eg_last + _dot_tn(kd, v_new)
    s_scr[...] = s
    o = outs[0] if len(outs) == 1 else jnp.concatenate(outs, axis=0)
    o = o * _rms_scale(o) * gain_ref[...]
    o = o * _silu(z_scr[...])
    o_ref[0] = o[0:tr].astype(o_ref.dtype)

    @pl.when(it == n_tiles - 1)
    def _():
        sout_ref[0, 0] = s


def _gdn(proj, off, conv_w, conv_hist8, s0, a_log, dt_bias, gdn_gain, tile):
    b, t, _ = proj.shape
    if t >= tile:
        tt = tr = tile
        n_tiles = t // tile
        t_valid = tt
    else:
        tt, tr, n_tiles, t_valid = GDN_CHUNK, t, 1, t
    cq, ck, cv, cz, cba = (off[n] // LANE for n in ("qa", "ka", "va", "za", "ba"))

    def tok(col0):
        return pl.BlockSpec((1, tr, LANE), lambda bi, hi, ti: (bi, ti, col0 + hi))

    def per_head(col0, rows):
        return pl.BlockSpec((rows, LANE), lambda bi, hi, ti: (0, col0 + hi))

    def hist(col0):
        return pl.BlockSpec((1, SUBLANE, LANE), lambda bi, hi, ti: (bi, 0, col0 + hi))

    smem = pl.BlockSpec(memory_space=pltpu.SMEM)
    body = functools.partial(_gdn_body, tt=tt, tr=tr, t_valid=t_valid, n_tiles=n_tiles)
    return pl.pallas_call(
        body,
        grid=(b, H_A, n_tiles),
        in_specs=[smem, smem,
                  tok(cq), tok(ck), tok(cv), tok(cz),
                  pl.BlockSpec((1, tr, LANE), lambda bi, hi, ti: (bi, ti, cba)),
                  per_head(0, CONV_K), per_head(H_A, CONV_K), per_head(2 * H_A, CONV_K),
                  hist(0), hist(H_A), hist(2 * H_A),
                  pl.BlockSpec((1, 1, DK_A, DK_A), lambda bi, hi, ti: (bi, hi, 0, 0)),
                  pl.BlockSpec((1, DK_A), lambda bi, hi, ti: (0, 0))],
        out_specs=[pl.BlockSpec((1, tr, LANE), lambda bi, hi, ti: (bi, ti, hi)),
                   pl.BlockSpec((1, 1, DK_A, DK_A), lambda bi, hi, ti: (bi, hi, 0, 0))],
        out_shape=[jax.ShapeDtypeStruct((b, t, QK_A), BF16),
                   jax.ShapeDtypeStruct((b, H_A, DK_A, DK_A), F32)],
        scratch_shapes=[pltpu.VMEM((DK_A, DK_A), F32),
                        pltpu.VMEM((3, tt + SUBLANE, LANE), F32),
                        pltpu.VMEM((tt, LANE), F32),
                        pltpu.VMEM((tt, LANE), F32)],
        compiler_params=_cparams(("parallel", "parallel", "arbitrary")),
        name="gdn",
    )(a_log, dt_bias, proj, proj, proj, proj, proj, conv_w, conv_w, conv_w,
      conv_hist8, conv_hist8, conv_hist8, s0, gdn_gain.reshape(1, DK_A))


def _attn_body(q_ref, k_ref, v_ref, o_ref, m_ref, l_ref, *, ls):
    j = N_BACK
    nt = ls // j
    nk = j if nt == 1 else 2 * j
    scale = HD_B ** -0.5

    def tile(i):
        if nt == 1:
            q0 = 0
            k0 = 0
        else:
            q0 = pl.multiple_of(i * j, j)
            k0 = pl.multiple_of(jnp.maximum(i - 1, 0) * j, j)
        q = q_ref[0, pl.ds(q0, j), :]
        kk = k_ref[0, pl.ds(k0, nk), :]
        vv = v_ref[0, pl.ds(k0, nk), :]
        s = _dot_nt(q, kk) * scale
        tq = q0 + lax.broadcasted_iota(jnp.int32, (j, nk), 0)
        tk = k0 + lax.broadcasted_iota(jnp.int32, (j, nk), 1)
        valid = jnp.logical_and(tk <= tq, tq - tk <= N_BACK)
        s = jnp.where(valid, s, -jnp.inf)
        m = jnp.max(s, axis=1, keepdims=True)
        p = jnp.exp(s - m)
        l = jnp.sum(p, axis=1, keepdims=True)
        o_ref[0, pl.ds(q0, j), :] = _dot(p, vv)
        m_ref[0, pl.ds(q0, j), :] = jnp.broadcast_to(m, (j, LANE))
        l_ref[0, pl.ds(q0, j), :] = jnp.broadcast_to(l, (j, LANE))

    if nt == 1:
        tile(0)
    else:
        def loop_body(i, carry):
            tile(i)
            return carry
        lax.fori_loop(0, nt, loop_body, 0)


def _attn_group(proj, off, gi, dil):
    b, s, npad = proj.shape
    ls = s // dil
    view = proj.reshape(b, ls, dil * npad)
    nblk = npad // LANE
    cq, ck, cv = (off[n] // LANE + gi * H_G for n in ("qb", "kb", "vb"))

    def spec(col0):
        return pl.BlockSpec((1, ls, LANE), lambda bi, ri, hi: (bi, 0, ri * nblk + col0 + hi))

    out_spec = pl.BlockSpec((1, ls, LANE), lambda bi, ri, hi: (bi, 0, ri * H_G + hi))
    out_sds = jax.ShapeDtypeStruct((b, ls, dil * OUT_B), F32)
    o, m, l = pl.pallas_call(
        functools.partial(_attn_body, ls=ls),
        grid=(b, dil, H_G),
        in_specs=[spec(cq), spec(ck), spec(cv)],
        out_specs=[out_spec, out_spec, out_spec],
        out_shape=[out_sds, out_sds, out_sds],
        compiler_params=_cparams(("parallel", "parallel", "parallel")),
        name=f"attn_g{gi}",
    )(view, view, view)
    return [a.reshape(b * s, OUT_B) for a in (o, m, l)]


def _attn_combine_body(o1, o2, o3, m1, m2, m3, l1, l2, l3, out_ref):
    m = jnp.maximum(jnp.maximum(m1[...], m2[...]), m3[...])
    e1 = jnp.exp(m1[...] - m)
    e2 = jnp.exp(m2[...] - m)
    e3 = jnp.exp(m3[...] - m)
    num = e1 * o1[...] + e2 * o2[...] + e3 * o3[...]
    den = e1 * l1[...] + e2 * l2[...] + e3 * l3[...]
    out_ref[...] = (num / den).astype(out_ref.dtype)


def _attn_combine(parts, tm):
    os_, ms, ls = zip(*parts)
    m = os_[0].shape[0]
    spec = pl.BlockSpec((tm, OUT_B), lambda i: (i, 0))
    return pl.pallas_call(
        _attn_combine_body,
        grid=(m // tm,),
        in_specs=[spec] * 9,
        out_specs=spec,
        out_shape=jax.ShapeDtypeStruct((m, OUT_B), BF16),
        compiler_params=_cparams(("parallel",)),
        name="attn_combine",
    )(*os_, *ms, *ls)


def _attn_sample_body(q_ref, kn_ref, vn_ref, c1_ref, c2_ref, c3_ref, o_ref, *, t_new):
    scale = HD_B ** -0.5
    q = q_ref[0]
    kn = kn_ref[0]
    vn = vn_ref[0]
    caches = (c1_ref, c2_ref, c3_ref)
    row_c = lax.broadcasted_iota(jnp.int32, (N_BACK, 1), 0)
    row_n = lax.broadcasted_iota(jnp.int32, (t_new, 1), 0)
    kv_row = 2 * OUT_B
    for t in range(t_new):
        for h in range(H_G):
            parts = []
            for gi in range(N_GROUPS):
                c0 = gi * OUT_B + h * HD_B
                qv = q[t:t + 1, c0:c0 + HD_B]
                base = 0 if gi == 0 else t * kv_row
                kc = caches[gi][0, :, base + h * HD_B:base + (h + 1) * HD_B]
                vc = caches[gi][0, :, base + OUT_B + h * HD_B:base + OUT_B + (h + 1) * HD_B]
                s_c = jnp.sum(kc * qv, axis=1, keepdims=True) * scale
                s_n = jnp.sum(kn[:, c0:c0 + HD_B] * qv, axis=1, keepdims=True) * scale
                if gi == 0:
                    s_c = jnp.where(row_c >= t, s_c, -jnp.inf)
                    s_n = jnp.where(row_n <= t, s_n, -jnp.inf)
                else:
                    s_n = jnp.where(row_n == t, s_n, -jnp.inf)
                m = jnp.maximum(jnp.max(s_c, axis=0, keepdims=True), jnp.max(s_n, axis=0, keepdims=True))
                p_c = jnp.exp(s_c - m)
                p_n = jnp.exp(s_n - m)
                l = jnp.sum(p_c, axis=0, keepdims=True) + jnp.sum(p_n, axis=0, keepdims=True)
                o = (jnp.sum(p_c * vc, axis=0, keepdims=True)
                     + jnp.sum(p_n * vn[:, c0:c0 + HD_B], axis=0, keepdims=True))
                parts.append((o, m, l))
            m_all = jnp.maximum(jnp.maximum(parts[0][1], parts[1][1]), parts[2][1])
            num = 0.0
            den = 0.0
            for (o, m, l) in parts:
                e = jnp.exp(m - m_all)
                num = num + e * o
                den = den + e * l
            o_ref[0, t:t + 1, h * HD_B:(h + 1) * HD_B] = (num / den).astype(o_ref.dtype)


def _attn_sample(proj, off, c1, c2, c3):
    b, t, _ = proj.shape
    kv_row = 2 * OUT_B
    views = [c1.reshape(b, N_BACK, kv_row),
             c2.reshape(b, N_BACK, DIL_GROUPS[1][1] * kv_row),
             c3.reshape(b, N_BACK, DIL_GROUPS[2][1] * kv_row)]
    widths = [kv_row, t * kv_row, t * kv_row]

    def pspec(name):
        return pl.BlockSpec((1, t, QKV_B), lambda bi: (bi, 0, off[name] // QKV_B))

    return pl.pallas_call(
        functools.partial(_attn_sample_body, t_new=t),
        grid=(b,),
        in_specs=[pspec("qb"), pspec("kb"), pspec("vb")]
        + [pl.BlockSpec((1, N_BACK, w), lambda bi: (bi, 0, 0)) for w in widths],
        out_specs=pl.BlockSpec((1, t, OUT_B), lambda bi: (bi, 0, 0)),
        out_shape=jax.ShapeDtypeStruct((b, t, OUT_B), BF16),
        compiler_params=_cparams(("parallel",)),
        name="attn_sample",
    )(proj, proj, proj, *views)


def _pool_body(u_ref, hist_ref, wp_ref, ps_ref, o_ref, ext_scr, *, tt, tr, pos0, n_tiles):
    hp = POOL_HIST + 1
    it = pl.program_id(1)

    @pl.when(it == 0)
    def _():
        ext_scr[0:hp, :] = hist_ref[0]

    if tr < tt:
        ext_scr[hp:, :] = jnp.zeros((tt, C_POOL), F32)
    ext_scr[hp:hp + tr, :] = u_ref[0]

    pos = pos0 + it * tt + lax.broadcasted_iota(jnp.int32, (tt, 1), 0)
    outs = []
    for gi, win in enumerate(POOL_WINDOWS):
        lo, hi = gi * CG, (gi + 1) * CG
        x = ext_scr[hp:hp + tt, lo:hi]
        acc = x
        for back in range(1, win):
            acc = acc + ext_scr[hp - back:hp - back + tt, lo:hi]
        cnt = jnp.minimum(win, pos + 1).astype(F32)
        pooled = acc / cnt - x
        outs.append(jnp.dot(pooled.astype(BF16), wp_ref[gi], preferred_element_type=F32))
    oc = jnp.concatenate(outs, axis=1) * ps_ref[...]
    o_ref[0] = oc[0:tr].astype(o_ref.dtype)

    if n_tiles > 1:
        ext_scr[0:hp, :] = ext_scr[tt:tt + hp, :]


def _pool(proj, off, hist16, w_pool, pool_scale, pos0, tile):
    b, t, _ = proj.shape
    hp = POOL_HIST + 1
    if t >= tile:
        tt = tr = tile
        n_tiles = t // tile
    else:
        tt, tr, n_tiles = SUBLANE, t, 1
    cu = off["uc"] // C_POOL
    assert off["uc"] % C_POOL == 0
    return pl.pallas_call(
        functools.partial(_pool_body, tt=tt, tr=tr, pos0=pos0, n_tiles=n_tiles),
        grid=(b, n_tiles),
        in_specs=[pl.BlockSpec((1, tr, C_POOL), lambda bi, ti: (bi, ti, cu)),
                  pl.BlockSpec((1, hp, C_POOL), lambda bi, ti: (bi, 0, 0)),
                  pl.BlockSpec((len(POOL_WINDOWS), CG, CG), lambda bi, ti: (0, 0, 0)),
                  pl.BlockSpec((1, C_POOL), lambda bi, ti: (0, 0))],
        out_specs=pl.BlockSpec((1, tr, C_POOL), lambda bi, ti: (bi, ti, 0)),
        out_shape=jax.ShapeDtypeStruct((b, t, C_POOL), BF16),
        scratch_shapes=[pltpu.VMEM((hp + tt, C_POOL), F32)],
        compiler_params=_cparams(("parallel", "arbitrary")),
        name="pool",
    )(proj, hist16, w_pool, pool_scale.reshape(1, C_POOL))


def _tile_rows(m, pref):
    return pref if m % pref == 0 else m


def _layer(x, wts, conv_hist, s0, pool_hist, caches, prompt):
    (w_in, conv_w, a_log, dt_bias, gdn_gain, w_pool, pool_scale, w_a, w_b, w_c, w_out, w_gu, w_down,
     g_pre_mix, g_post_mix, g_pre_ffn, g_post_ffn) = wts
    b, t, d = x.shape
    m = b * t
    off, npad = _proj_layout(d)
    x2 = x.reshape(m, d)
    tm = _tile_rows(m, 512)

    proj2 = _norm_matmul(x2, g_pre_mix, w_in, tm, 512)
    proj = proj2.reshape(b, t, npad)

    hist8 = jnp.concatenate([jnp.zeros((b, SUBLANE - (CONV_K - 1), 3 * QK_A), F32), conv_hist], axis=1)
    out_a, s_new = _gdn(proj, off, conv_w, hist8, s0, a_log, dt_bias, gdn_gain, GDN_TILE)
    qa0 = off["qa"]
    conv_new = jnp.concatenate([conv_hist, proj[:, :, qa0:qa0 + 3 * QK_A]], axis=1)[:, -(CONV_K - 1):]

    kb0, vb0 = off["kb"], off["vb"]
    if prompt:
        parts = [_attn_group(proj, off, gi, dil) for gi, (_, dil) in enumerate(DIL_GROUPS)]
        out_b = _attn_combine(parts, tm)
    else:
        out_b = _attn_sample(proj, off, *caches).reshape(m, OUT_B)
    win_new = []
    for gi, (win, _) in enumerate(DIL_GROUPS):
        kg = proj[:, :, kb0 + gi * OUT_B:kb0 + (gi + 1) * OUT_B].reshape(b, t, 1, H_G, HD_B)
        vg = proj[:, :, vb0 + gi * OUT_B:vb0 + (gi + 1) * OUT_B].reshape(b, t, 1, H_G, HD_B)
        kv = jnp.concatenate([kg, vg], axis=2)
        if not prompt:
            kv = jnp.concatenate([caches[gi], kv], axis=1)
        win_new.append(kv[:, -win:])

    hist16 = jnp.concatenate([jnp.zeros((b, 1, C_POOL), F32), pool_hist], axis=1)
    out_c = _pool(proj, off, hist16, w_pool, pool_scale, 0 if prompt else SAMPLE_POS0, 256)
    uc0 = off["uc"]
    pool_new = jnp.concatenate([pool_hist, proj[:, :, uc0:uc0 + C_POOL]], axis=1)[:, -POOL_HIST:]

    merged = _merge(out_a.reshape(m, QK_A), out_b, out_c.reshape(m, C_POOL), w_a, w_b, w_c, proj2, off, tm, 512)
    x2 = _mm_norm_res(merged, w_out, x2, g_post_mix, tm, 512)
    act = _ffn_up(x2, g_pre_ffn, w_gu, tm, 512)
    x2 = _mm_norm_res(act, w_down, x2, g_post_ffn, tm, 512)
    return x2.reshape(b, t, d), (win_new[0], win_new[1], win_new[2], s_new, conv_new, pool_new)


def kernel(x_prompt, x_sample, cache_win1, cache_win2, cache_win3, state_gdn, state_conv, state_pool,
           w_in, conv_w, a_log, dt_bias, gdn_gain, w_pool, pool_scale, w_br_a, w_br_b, w_br_c,
           w_out, w_gu, w_down, g_pre_mix, g_post_mix, g_pre_ffn, g_post_ffn):
    depth = w_in.shape[0]
    bp = x_prompt.shape[0]
    d = x_prompt.shape[-1]
    yp, ys = x_prompt, x_sample
    new_p, new_s = [], []
    for l in range(depth):
        wts = (_prep_w_in(w_in[l], d), conv_w[l], a_log[l], dt_bias[l], gdn_gain[l],
               w_pool[l].astype(BF16), pool_scale[l],
               w_br_a[l].astype(BF16), w_br_b[l].astype(BF16), w_br_c[l].astype(BF16),
               w_out[l].astype(BF16), w_gu[l].astype(BF16), w_down[l].astype(BF16),
               g_pre_mix[l], g_post_mix[l], g_pre_ffn[l], g_post_ffn[l])
        yp, st_p = _layer(yp, wts,
                          jnp.zeros((bp, CONV_K - 1, 3 * QK_A), F32),
                          jnp.zeros((bp, H_A, DK_A, DK_A), F32),
                          jnp.zeros((bp, POOL_HIST, C_POOL), F32),
                          None, True)
        ys, st_s = _layer(ys, wts, state_conv[l], state_gdn[l], state_pool[l],
                          (cache_win1[l], cache_win2[l], cache_win3[l]), False)
        new_p.append(st_p)
        new_s.append(st_s)
    outs_p = [jnp.stack([st[i] for st in new_p], axis=0) for i in range(6)]
    outs_s = [jnp.stack([st[i] for st in new_s], axis=0) for i in range(6)]
    return (yp, ys, *outs_p, *outs_s)
```

```python
import functools

import jax
import jax.numpy as jnp
from jax import lax
from jax.experimental import pallas as pl
from jax.experimental.pallas import tpu as pltpu

F32 = jnp.float32
BF16 = jnp.bfloat16

H_A = 8
DK_A = 128
CONV_K = 4
DIL_GROUPS = ((128, 1), (512, 4), (2048, 16))
N_GROUPS = len(DIL_GROUPS)
H_G = 4
HD_B = 128
N_BACK = 128
POOL_WINDOWS = (2, 4, 8, 16)
CG = 256
C_POOL = CG * len(POOL_WINDOWS)
POOL_HIST = max(POOL_WINDOWS) - 1
SAMPLE_POS0 = 8192
EPS = 1e-6
L2_EPS = 1e-6

LANE = 128
SUBLANE = 8
VMEM_LIMIT = 48 * 1024 * 1024

QK_A = H_A * DK_A
QKV_B = N_GROUPS * H_G * HD_B
OUT_B = H_G * HD_B
Q_TILE = 128


def _proj_layout(d_model):
    off = {}
    c = 0
    for name, width in (("qb", QKV_B), ("kb", QKV_B), ("vb", QKV_B),
                        ("ba", H_A), ("aa", H_A), ("pad0", 512 - 2 * H_A),
                        ("uc", C_POOL),
                        ("qa", QK_A), ("ka", QK_A), ("va", QK_A), ("za", QK_A),
                        ("ga", d_model), ("gb", d_model), ("gc", d_model)):
        off[name] = c
        c += width
    total = -(-c // 512) * 512
    return off, total


def _prep_w_in(w, d_model):
    sizes = (QK_A, QK_A, QK_A, QK_A, H_A, H_A, QKV_B, QKV_B, QKV_B, C_POOL, d_model, d_model, d_model)
    cuts = [0]
    for s in sizes:
        cuts.append(cuts[-1] + s)
    qa, ka, va, za, ba, aa, qb, kb, vb, uc, ga, gb, gc = [w[..., cuts[i]:cuts[i + 1]] for i in range(len(sizes))]
    _, total = _proj_layout(d_model)
    pad0 = jnp.zeros(w.shape[:-1] + (512 - 2 * H_A,), w.dtype)
    parts = [qb, kb, vb, ba, aa, pad0, uc, qa, ka, va, za, ga, gb, gc]
    used = sum(p.shape[-1] for p in parts)
    if total > used:
        parts.append(jnp.zeros(w.shape[:-1] + (total - used,), w.dtype))
    return jnp.concatenate(parts, axis=-1).astype(BF16)


def _cparams(sem):
    return pltpu.CompilerParams(dimension_semantics=sem, vmem_limit_bytes=VMEM_LIMIT)


def _dot(a, b):
    return jnp.dot(a.astype(BF16), b.astype(BF16), preferred_element_type=F32)


def _dot_nt(a, b):
    return lax.dot_general(a.astype(BF16), b.astype(BF16), (((1,), (1,)), ((), ())),
                           preferred_element_type=F32)


def _dot_tn(a, b):
    return lax.dot_general(a.astype(BF16), b.astype(BF16), (((0,), (0,)), ((), ())),
                           preferred_element_type=F32)


def _split3(x):
    x1 = x.astype(BF16).astype(F32)
    r1 = x - x1
    x2 = r1.astype(BF16).astype(F32)
    x3 = (r1 - x2).astype(BF16).astype(F32)
    return x1, x2, x3


def _rms_scale(y):
    return lax.rsqrt(jnp.mean(y * y, axis=-1, keepdims=True) + EPS)


def _silu(x):
    return x * jax.nn.sigmoid(x)


def _wspec(l, rows, cols, index):
    return pl.BlockSpec((None, rows, cols), lambda *g: (l,) + index(*g))


def _norm_mm_body(x_ref, g_ref, w_ref, o_ref, h_ref):
    @pl.when(pl.program_id(1) == 0)
    def _():
        xf = x_ref[...]
        h_ref[...] = (xf * _rms_scale(xf) * g_ref[...]).astype(BF16)

    o_ref[...] = jnp.dot(h_ref[...], w_ref[...], preferred_element_type=F32)


def _norm_matmul(x, gain, w, l, tm, tn):
    m, d = x.shape
    n = w.shape[-1]
    return pl.pallas_call(
        _norm_mm_body,
        grid=(m // tm, n // tn),
        in_specs=[pl.BlockSpec((tm, d), lambda i, j: (i, 0)),
                  _wspec(l, 1, d, lambda i, j: (0, 0)),
                  _wspec(l, d, tn, lambda i, j: (0, j))],
        out_specs=pl.BlockSpec((tm, tn), lambda i, j: (i, j)),
        out_shape=jax.ShapeDtypeStruct((m, n), F32),
        scratch_shapes=[pltpu.VMEM((tm, d), BF16)],
        compiler_params=_cparams(("parallel", "arbitrary")),
        name="in_proj",
    )(x, gain, w)


def _ffn_up_body(x_ref, g_ref, wg_ref, wu_ref, o_ref, h_ref):
    @pl.when(pl.program_id(1) == 0)
    def _():
        xf = x_ref[...]
        h_ref[...] = (xf * _rms_scale(xf) * g_ref[...]).astype(BF16)

    h = h_ref[...]
    gate = jnp.dot(h, wg_ref[...], preferred_element_type=F32)
    up = jnp.dot(h, wu_ref[...], preferred_element_type=F32)
    o_ref[...] = (_silu(gate) * up).astype(BF16)


def _ffn_up(x, gain, w_gu, l, tm, tn):
    m, d = x.shape
    d_ff = w_gu.shape[-1] // 2
    nj = d_ff // tn
    return pl.pallas_call(
        _ffn_up_body,
        grid=(m // tm, nj),
        in_specs=[pl.BlockSpec((tm, d), lambda i, j: (i, 0)),
                  _wspec(l, 1, d, lambda i, j: (0, 0)),
                  _wspec(l, d, tn, lambda i, j: (0, j)),
                  _wspec(l, d, tn, lambda i, j: (0, j + nj))],
        out_specs=pl.BlockSpec((tm, tn), lambda i, j: (i, j)),
        out_shape=jax.ShapeDtypeStruct((m, d_ff), BF16),
        scratch_shapes=[pltpu.VMEM((tm, d), BF16)],
        compiler_params=_cparams(("parallel", "arbitrary")),
        name="ffn_up",
    )(x, gain, w_gu, w_gu)


def _mm_norm_res_body(a_ref, w_ref, res_ref, g_ref, o_ref, acc_ref, *, nk):
    k = pl.program_id(1)

    @pl.when(k == 0)
    def _():
        acc_ref[...] = jnp.zeros_like(acc_ref)

    acc_ref[...] += jnp.dot(a_ref[...], w_ref[...], preferred_element_type=F32)

    @pl.when(k == nk - 1)
    def _():
        y = acc_ref[...]
        o_ref[...] = res_ref[...] + y * _rms_scale(y) * g_ref[...]


def _mm_norm_res(a, w, res, gain, l, tm, tk):
    m, kdim = a.shape
    d = w.shape[-1]
    nk = kdim // tk
    return pl.pallas_call(
        functools.partial(_mm_norm_res_body, nk=nk),
        grid=(m // tm, nk),
        in_specs=[pl.BlockSpec((tm, tk), lambda i, k: (i, k)),
                  _wspec(l, tk, d, lambda i, k: (k, 0)),
                  pl.BlockSpec((tm, d), lambda i, k: (i, 0)),
                  _wspec(l, 1, d, lambda i, k: (0, 0))],
        out_specs=pl.BlockSpec((tm, d), lambda i, k: (i, 0)),
        out_shape=jax.ShapeDtypeStruct((m, d), F32),
        scratch_shapes=[pltpu.VMEM((tm, d), F32)],
        compiler_params=_cparams(("parallel", "arbitrary")),
        name="mm_norm_res",
    )(a, w, res, gain)


def _merge_body(a_ref, b_ref, c_ref, wa_ref, wb_ref, wc_ref, ga_ref, gb_ref, gc_ref, o_ref):
    m = jax.nn.sigmoid(ga_ref[...]) * jnp.dot(a_ref[...], wa_ref[...], preferred_element_type=F32)
    m = m + jax.nn.sigmoid(gb_ref[...]) * jnp.dot(b_ref[...], wb_ref[...], preferred_element_type=F32)
    m = m + jax.nn.sigmoid(gc_ref[...]) * jnp.dot(c_ref[...], wc_ref[...], preferred_element_type=F32)
    o_ref[...] = m.astype(BF16)


def _merge(out_a, out_b, out_c, w_a, w_b, w_c, proj2d, off, l, tm, tn):
    m = out_a.shape[0]
    d = w_a.shape[-1]
    ja, jb, jc = off["ga"] // tn, off["gb"] // tn, off["gc"] // tn
    return pl.pallas_call(
        _merge_body,
        grid=(m // tm, d // tn),
        in_specs=[pl.BlockSpec((tm, out_a.shape[1]), lambda i, j: (i, 0)),
                  pl.BlockSpec((tm, out_b.shape[1]), lambda i, j: (i, 0)),
                  pl.BlockSpec((tm, out_c.shape[1]), lambda i, j: (i, 0)),
                  _wspec(l, w_a.shape[1], tn, lambda i, j: (0, j)),
                  _wspec(l, w_b.shape[1], tn, lambda i, j: (0, j)),
                  _wspec(l, w_c.shape[1], tn, lambda i, j: (0, j)),
                  pl.BlockSpec((tm, tn), lambda i, j: (i, j + ja)),
                  pl.BlockSpec((tm, tn), lambda i, j: (i, j + jb)),
                  pl.BlockSpec((tm, tn), lambda i, j: (i, j + jc))],
        out_specs=pl.BlockSpec((tm, tn), lambda i, j: (i, j)),
        out_shape=jax.ShapeDtypeStruct((m, d), BF16),
        compiler_params=_cparams(("parallel", "arbitrary")),
        name="merge",
    )(out_a, out_b, out_c, w_a, w_b, w_c, proj2d, proj2d, proj2d)


def _unit_lower_inverse(lower, c):
    row = lax.broadcasted_iota(jnp.int32, (c, c), 0)
    col = lax.broadcasted_iota(jnp.int32, (c, c), 1)

    def same_block(shift):
        return (row >> shift) == (col >> shift)

    eye = jnp.where(row == col, 1.0, 0.0).astype(F32)
    d1 = jnp.where(same_block(3), lower, 0.0)
    d2 = _dot(d1, d1)
    d4 = _dot(d2, d2)
    x = eye - d1
    x = x + _dot(d2, x)
    x = x + _dot(d4, x)
    shift = 3
    while (1 << shift) < c:
        off_diag = jnp.where(jnp.logical_and(same_block(shift + 1), jnp.logical_not(same_block(shift))),
                             lower, 0.0)
        x = x - _dot(_dot(x, off_diag), x)
        shift += 1
    return x


def _gdn_body(alog_ref, dtb_ref, q_ref, k_ref, v_ref, z_ref, ba_ref, cwq_ref, cwk_ref, cwv_ref,
              hq_ref, hk_ref, hv_ref, s0_ref, gain_ref, o_ref, sout_ref,
              s_scr, ext_scr, z_scr, ba_scr, *, tt, tr, t_valid, n_tiles, chunk, hb):
    c = chunk
    h0 = pl.program_id(1) * hb
    it = pl.program_id(2)
    wid = hb * LANE

    @pl.when(it == 0)
    def _():
        s_scr[...] = s0_ref[0]
        ext_scr[0, 0:SUBLANE, :] = hq_ref[0]
        ext_scr[1, 0:SUBLANE, :] = hk_ref[0]
        ext_scr[2, 0:SUBLANE, :] = hv_ref[0]

    if tr < tt:
        for i in range(3):
            ext_scr[i, SUBLANE:, :] = jnp.zeros((tt, wid), F32)
        z_scr[...] = jnp.zeros((tt, wid), F32)
        ba_scr[...] = jnp.zeros((tt, LANE), F32)
        z_scr[0:tr, :] = z_ref[0]
        ba_scr[0:tr, :] = ba_ref[0]
        z_all = z_scr[...]
        ba = ba_scr[...]
    else:
        z_all = z_ref[0]
        ba = ba_ref[0]
    ext_scr[0, SUBLANE:SUBLANE + tr, :] = q_ref[0]
    ext_scr[1, SUBLANE:SUBLANE + tr, :] = k_ref[0]
    ext_scr[2, SUBLANE:SUBLANE + tr, :] = v_ref[0]

    def conv_act(i, cw_ref):
        base = SUBLANE - (CONV_K - 1)
        y = ext_scr[i, base:base + tt, :] * cw_ref[0:1, :]
        for j in range(1, CONV_K):
            y = y + ext_scr[i, base + j:base + j + tt, :] * cw_ref[j:j + 1, :]
        return _silu(y)

    q_all = conv_act(0, cwq_ref)
    k_all = conv_act(1, cwk_ref)
    v_all = conv_act(2, cwv_ref)

    if n_tiles > 1:
        for i in range(3):
            ext_scr[i, 0:SUBLANE, :] = ext_scr[i, tt:tt + SUBLANE, :]

    lane = lax.broadcasted_iota(jnp.int32, (tt, LANE), 1)
    valid = lax.broadcasted_iota(jnp.int32, (tt, 1), 0) < t_valid
    row = lax.broadcasted_iota(jnp.int32, (c, c), 0)
    col = lax.broadcasted_iota(jnp.int32, (c, c), 1)
    incl = row >= col
    strict = row > col
    eye = row == col
    tri = jnp.where(incl, 1.0, 0.0).astype(BF16)

    for hh in range(hb):
        hsl = slice(hh * LANE, (hh + 1) * LANE)
        q = q_all[:, hsl]
        k = k_all[:, hsl]
        v = v_all[:, hsl]
        q = q * lax.rsqrt(jnp.sum(q * q, axis=-1, keepdims=True) + L2_EPS) * (DK_A ** -0.5)
        k = k * lax.rsqrt(jnp.sum(k * k, axis=-1, keepdims=True) + L2_EPS)

        b_raw = jnp.sum(jnp.where(lane == h0 + hh, ba, 0.0), axis=1, keepdims=True)
        a_raw = jnp.sum(jnp.where(lane == h0 + hh + H_A, ba, 0.0), axis=1, keepdims=True)
        beta = jax.nn.sigmoid(b_raw)
        a_log = jnp.full((1, 1), alog_ref[h0 + hh], F32)
        xs = a_raw + dtb_ref[h0 + hh]
        softplus = jnp.maximum(xs, 0.0) + jnp.log1p(jnp.exp(-jnp.abs(xs)))
        g = -jnp.exp(a_log) * softplus

        if t_valid < tt:
            q = jnp.where(valid, q, 0.0)
            k = jnp.where(valid, k, 0.0)
            v = jnp.where(valid, v, 0.0)
            beta = jnp.where(valid, beta, 0.0)
            g = jnp.where(valid, g, 0.0)

        pre = []
        for ci in range(tt // c):
            sl = slice(ci * c, (ci + 1) * c)
            qc, kc, vc, bc, gcol = q[sl], k[sl], v[sl], beta[sl], g[sl]
            g1, g2, g3 = _split3(jnp.broadcast_to(gcol, (c, LANE)))
            lane_c = lax.broadcasted_iota(jnp.int32, (c, LANE), 1)
            g_terms = jnp.where(lane_c == 0, g1, jnp.where(lane_c == 1, g2, g3))
            part = jnp.dot(tri, g_terms.astype(BF16), preferred_element_type=F32)
            gc_col0 = part[:, 0:1] + part[:, 1:2] + part[:, 2:3]
            gc_cb = jnp.broadcast_to(gc_col0, (c, c))
            gc_rb = jnp.sum(jnp.where(eye, gc_cb, 0.0), axis=0, keepdims=True)
            decay = jnp.where(incl, jnp.exp(jnp.where(incl, gc_cb - gc_rb, 0.0)), 0.0)
            kb = kc * bc
            lower = jnp.where(strict, _dot_nt(kb, kc) * decay, 0.0)
            tinv = _unit_lower_inverse(lower, c)
            gc_col = gc_cb[:, 0:1]
            egc = jnp.exp(gc_col)
            u = _dot(tinv, vc * bc)
            w = _dot(tinv, kb * egc)
            a_intra = jnp.where(incl, _dot_nt(qc, kc) * decay, 0.0)
            g_last = gc_cb[c - 1:c, 0:1]
            kd = kc * jnp.exp(g_last - gc_col)
            pre.append((u, w, a_intra, kd, qc * egc, jnp.exp(g_last)))

        s = s_scr[hh]
        outs = []
        for (u, w, a_intra, kd, qe, eg_last) in pre:
            v_new = u - _dot(w, s)
            outs.append(_dot(qe, s) + _dot(a_intra, v_new))
            s = s * eg_last + _dot_tn(kd, v_new)
        s_scr[hh] = s
        o = outs[0] if len(outs) == 1 else jnp.concatenate(outs, axis=0)
        o = o * _rms_scale(o) * gain_ref[...]
        o = o * _silu(z_all[:, hsl])
        o_ref[0, :, hsl] = o[0:tr].astype(o_ref.dtype)

    @pl.when(it == n_tiles - 1)
    def _():
        sout_ref[0] = s_scr[...]


def _gdn(proj, off, conv_w, conv_hist8, s0, a_log, dt_bias, gdn_gain, l, *, tile, chunk, hb):
    b, t, _ = proj.shape
    if t >= tile:
        tt = tr = tile
        n_tiles = t // tile
    else:
        tt, tr, n_tiles = chunk, t, 1
    wid = hb * LANE
    cq, ck, cv, cz = (off[n] // wid for n in ("qa", "ka", "va", "za"))
    cba = off["ba"] // LANE
    nhb = H_A // hb

    def tok(col0):
        return pl.BlockSpec((1, tr, wid), lambda bi, hi, ti: (bi, ti, col0 + hi))

    def per_head(col0, rows):
        return _wspec(l, rows, wid, lambda bi, hi, ti: (0, col0 + hi))

    def hist(col0):
        return pl.BlockSpec((1, SUBLANE, wid), lambda bi, hi, ti: (bi, 0, col0 + hi))

    smem = pl.BlockSpec(memory_space=pltpu.SMEM)
    state_spec = pl.BlockSpec((1, hb, DK_A, DK_A), lambda bi, hi, ti: (bi, hi, 0, 0))
    body = functools.partial(_gdn_body, tt=tt, tr=tr, t_valid=t, n_tiles=n_tiles, chunk=chunk, hb=hb)
    return pl.pallas_call(
        body,
        grid=(b, nhb, n_tiles),
        in_specs=[smem, smem,
                  tok(cq), tok(ck), tok(cv), tok(cz),
                  pl.BlockSpec((1, tr, LANE), lambda bi, hi, ti: (bi, ti, cba)),
                  per_head(0, CONV_K), per_head(nhb, CONV_K), per_head(2 * nhb, CONV_K),
                  hist(0), hist(nhb), hist(2 * nhb),
                  state_spec,
                  _wspec(l, 1, DK_A, lambda bi, hi, ti: (0, 0))],
        out_specs=[pl.BlockSpec((1, tr, wid), lambda bi, hi, ti: (bi, ti, hi)),
                   state_spec],
        out_shape=[jax.ShapeDtypeStruct((b, t, QK_A), BF16),
                   jax.ShapeDtypeStruct((b, H_A, DK_A, DK_A), F32)],
        scratch_shapes=[pltpu.VMEM((hb, DK_A, DK_A), F32),
                        pltpu.VMEM((3, tt + SUBLANE, wid), F32),
                        pltpu.VMEM((tt, wid), F32),
                        pltpu.VMEM((tt, LANE), F32)],
        compiler_params=_cparams(("parallel", "parallel", "arbitrary")),
        name="gdn",
    )(a_log[l], dt_bias[l], proj, proj, proj, proj, proj, conv_w, conv_w, conv_w,
      conv_hist8, conv_hist8, conv_hist8, s0, gdn_gain)


def _attn_body(q0_ref, q1_ref, q2_ref, k0_ref, k1_ref, k2_ref, v0_ref, v1_ref, v2_ref, o_ref,
               kb_scr, vb_scr, *, s_len):
    q_refs = (q0_ref, q1_ref, q2_ref)
    k_refs = (k0_ref, k1_ref, k2_ref)
    v_refs = (v0_ref, v1_ref, v2_ref)
    tq_n = Q_TILE
    scale = HD_B ** -0.5
    for gi in range(N_GROUPS):
        kb_scr[gi] = k_refs[gi][0].astype(BF16)
        vb_scr[gi] = v_refs[gi][0].astype(BF16)

    def tile(i, carry):
        q0 = pl.multiple_of(i * tq_n, tq_n)
        scores = []
        starts = []
        m = None
        for gi, (win, dil) in enumerate(DIL_GROUPS):
            nk = min(s_len, win + tq_n)
            k0 = pl.multiple_of(jnp.clip(q0 - win, 0, s_len - nk), tq_n)
            q = q_refs[gi][0, pl.ds(q0, tq_n), :]
            s = _dot_nt(q, kb_scr[gi, pl.ds(k0, nk), :]) * scale
            d = ((q0 - k0) + lax.broadcasted_iota(jnp.int32, (tq_n, nk), 0)
                 - lax.broadcasted_iota(jnp.int32, (tq_n, nk), 1))
            ok = jnp.logical_and(jnp.logical_and(d >= 0, d <= win), (d & (dil - 1)) == 0)
            s = jnp.where(ok, s, -jnp.inf)
            mg = jnp.max(s, axis=1, keepdims=True)
            m = mg if m is None else jnp.maximum(m, mg)
            scores.append(s)
            starts.append((k0, nk))
        acc = jnp.zeros((tq_n, HD_B), F32)
        den = jnp.zeros((tq_n, 1), F32)
        for gi in range(N_GROUPS):
            k0, nk = starts[gi]
            p = jnp.exp(scores[gi] - m)
            den = den + jnp.sum(p, axis=1, keepdims=True)
            acc = acc + _dot(p, vb_scr[gi, pl.ds(k0, nk), :])
        o_ref[0, pl.ds(q0, tq_n), :] = (acc / den).astype(o_ref.dtype)
        return carry

    lax.fori_loop(0, s_len // tq_n, tile, 0)


def _attn_prompt(proj, off):
    b, s_len, _ = proj.shape
    cq, ck, cv = (off[n] // LANE for n in ("qb", "kb", "vb"))

    def spec(col0, gi):
        return pl.BlockSpec((1, s_len, LANE), lambda bi, hi: (bi, 0, col0 + gi * H_G + hi))

    return pl.pallas_call(
        functools.partial(_attn_body, s_len=s_len),
        grid=(b, H_G),
        in_specs=[spec(c0, gi) for c0 in (cq, ck, cv) for gi in range(N_GROUPS)],
        out_specs=pl.BlockSpec((1, s_len, LANE), lambda bi, hi: (bi, 0, hi)),
        out_shape=jax.ShapeDtypeStruct((b, s_len, OUT_B), BF16),
        scratch_shapes=[pltpu.VMEM((N_GROUPS, s_len, HD_B), BF16),
                        pltpu.VMEM((N_GROUPS, s_len, HD_B), BF16)],
        compiler_params=_cparams(("parallel", "parallel")),
        name="attn_prompt",
    )(*([proj] * 9))


SHIFT_BATCH_CHUNKS = 4


def _shift_body(c1_ref, c2_ref, c3_ref, o1_ref, o2_ref, o3_ref, sem, tail_sem, *, t_new):
    copies = []
    for gi, (c_ref, o_ref) in enumerate(((c1_ref, o1_ref), (c2_ref, o2_ref), (c3_ref, o3_ref))):
        depth, batch, win = c_ref.shape[:3]
        nb = batch // SHIFT_BATCH_CHUNKS
        for l in range(depth):
            for ch in range(SHIFT_BATCH_CHUNKS):
                copies.append(pltpu.make_async_copy(
                    c_ref.at[l, pl.ds(ch * nb, nb), pl.ds(t_new, win - t_new)],
                    o_ref.at[l, pl.ds(ch * nb, nb), pl.ds(0, win - t_new)],
                    sem.at[gi, l, ch]))
            copies.append(pltpu.make_async_copy(
                c_ref.at[l, :, pl.ds(win - t_new, t_new)],
                o_ref.at[l, :, pl.ds(win - t_new, t_new)],
                tail_sem.at[gi, l]))
    for cp in copies:
        cp.start()
    for cp in copies:
        cp.wait()


def _shift_caches(c1, c2, c3, t_new):
    depth, batch = c1.shape[:2]
    assert batch % SHIFT_BATCH_CHUNKS == 0
    any_spec = pl.BlockSpec(memory_space=pl.ANY)
    return pl.pallas_call(
        functools.partial(_shift_body, t_new=t_new),
        in_specs=[any_spec] * 3,
        out_specs=[any_spec] * 3,
        out_shape=[jax.ShapeDtypeStruct(c.shape, c.dtype) for c in (c1, c2, c3)],
        scratch_shapes=[pltpu.SemaphoreType.DMA((N_GROUPS, depth, SHIFT_BATCH_CHUNKS)),
                        pltpu.SemaphoreType.DMA((N_GROUPS, depth))],
        name="cache_shift",
    )(c1, c2, c3)


def _attn_sample_body(q_ref, kn_ref, vn_ref, c1_ref, c2_ref, c3_ref, w1_in, w2_in, w3_in,
                      o_ref, w1_ref, w2_ref, w3_ref, *, t_new):
    del w1_in, w2_in, w3_in
    scale = HD_B ** -0.5
    q = q_ref[0]
    kn = kn_ref[0]
    vn = vn_ref[0]
    caches = (c1_ref, c2_ref, c3_ref)
    wins = (w1_ref, w2_ref, w3_ref)
    row_c = lax.broadcasted_iota(jnp.int32, (N_BACK, H_G, 1), 0)

    def heads(x, t, gi):
        c0 = gi * OUT_B
        return jnp.concatenate([x[t:t + 1, c0 + h * HD_B:c0 + (h + 1) * HD_B] for h in range(H_G)], axis=0)

    k_new = [[heads(kn, t, gi) for t in range(t_new)] for gi in range(N_GROUPS)]
    v_new = [[heads(vn, t, gi) for t in range(t_new)] for gi in range(N_GROUPS)]
    for gi in range(N_GROUPS):
        for t in range(t_new):
            wins[gi][0, t, 0] = k_new[gi][t]
            wins[gi][0, t, 1] = v_new[gi][t]

    for t in range(t_new):
        parts = []
        for gi in range(N_GROUPS):
            qv = heads(q, t, gi)
            r = 0 if gi == 0 else t
            kc = caches[gi][0, :, r, 0]
            vc = caches[gi][0, :, r, 1]
            s_c = jnp.sum(kc * qv[None], axis=-1, keepdims=True) * scale
            if gi == 0:
                s_c = jnp.where(row_c >= t, s_c, -jnp.inf)
                news = list(range(t + 1))
            else:
                news = [t]
            s_n = [jnp.sum(k_new[gi][tn] * qv, axis=-1, keepdims=True) * scale for tn in news]
            m = jnp.max(s_c, axis=0)
            for sn in s_n:
                m = jnp.maximum(m, sn)
            p_c = jnp.exp(s_c - m[None])
            l = jnp.sum(p_c, axis=0)
            o = jnp.sum(p_c * vc, axis=0)
            for tn, sn in zip(news, s_n):
                p_n = jnp.exp(sn - m)
                l = l + p_n
                o = o + p_n * v_new[gi][tn]
            parts.append((o, m, l))
        m_all = jnp.maximum(jnp.maximum(parts[0][1], parts[1][1]), parts[2][1])
        num = 0.0
        den = 0.0
        for (o, m, l) in parts:
            e = jnp.exp(m - m_all)
            num = num + e * o
            den = den + e * l
        out = (num / den).astype(o_ref.dtype)
        for h in range(H_G):
            o_ref[0, t:t + 1, h * HD_B:(h + 1) * HD_B] = out[h:h + 1, :]


def _attn_sample(proj, off, caches, windows, l):
    b, t, _ = proj.shape
    depth = caches[0].shape[0]
    assert t <= DIL_GROUPS[1][1] and DIL_GROUPS[0][1] == 1
    views, cspecs = [], []
    for c, (win, dil) in zip(caches, DIL_GROUPS):
        views.append(c.reshape(depth, b, N_BACK, dil, 2, H_G, HD_B))
        rows = min(dil, t)
        cspecs.append(pl.BlockSpec((None, 1, N_BACK, rows, 2, H_G, HD_B),
                                   lambda bi: (l, bi, 0, 0, 0, 0, 0)))
    wspecs = [pl.BlockSpec((None, 1, t, 2, H_G, HD_B),
                           functools.partial(lambda bi, blk: (l, bi, blk, 0, 0, 0), blk=win // t - 1))
              for (win, _) in DIL_GROUPS]

    def pspec(name):
        return pl.BlockSpec((1, t, QKV_B), lambda bi: (bi, 0, off[name] // QKV_B))

    any_spec = pl.BlockSpec(memory_space=pl.ANY)
    res = pl.pallas_call(
        functools.partial(_attn_sample_body, t_new=t),
        grid=(b,),
        in_specs=[pspec("qb"), pspec("kb"), pspec("vb")] + cspecs + [any_spec] * 3,
        out_specs=[pl.BlockSpec((1, t, OUT_B), lambda bi: (bi, 0, 0))] + wspecs,
        out_shape=[jax.ShapeDtypeStruct((b, t, OUT_B), BF16)]
        + [jax.ShapeDtypeStruct(w.shape, w.dtype) for w in windows],
        input_output_aliases={6: 1, 7: 2, 8: 3},
        compiler_params=_cparams(("parallel",)),
        name="attn_sample",
    )(proj, proj, proj, *views, *windows)
    return res[0], tuple(res[1:])


def _pool_body(u_ref, hist_ref, wp_ref, ps_ref, o_ref, ext_scr, *, tt, tr, pos0, n_tiles):
    hp = POOL_HIST + 1
    it = pl.program_id(1)

    @pl.when(it == 0)
    def _():
        ext_scr[0:hp, :] = hist_ref[0]

    if tr < tt:
        ext_scr[hp:, :] = jnp.zeros((tt, C_POOL), F32)
    ext_scr[hp:hp + tr, :] = u_ref[0]

    pos = pos0 + it * tt + lax.broadcasted_iota(jnp.int32, (tt, 1), 0)
    outs = []
    for gi, win in enumerate(POOL_WINDOWS):
        lo, hi = gi * CG, (gi + 1) * CG
        x = ext_scr[hp:hp + tt, lo:hi]
        acc = x
        for back in range(1, win):
            acc = acc + ext_scr[hp - back:hp - back + tt, lo:hi]
        cnt = jnp.minimum(win, pos + 1).astype(F32)
        pooled = acc / cnt - x
        outs.append(jnp.dot(pooled.astype(BF16), wp_ref[gi], preferred_element_type=F32))
    oc = jnp.concatenate(outs, axis=1) * ps_ref[...]
    o_ref[0] = oc[0:tr].astype(o_ref.dtype)

    if n_tiles > 1:
        ext_scr[0:hp, :] = ext_scr[tt:tt + hp, :]


def _pool(proj, off, hist16, w_pool, pool_scale, l, pos0, tile):
    b, t, _ = proj.shape
    hp = POOL_HIST + 1
    if t >= tile:
        tt = tr = tile
        n_tiles = t // tile
    else:
        tt, tr, n_tiles = SUBLANE, t, 1
    cu = off["uc"] // C_POOL
    assert off["uc"] % C_POOL == 0
    ng = len(POOL_WINDOWS)
    return pl.pallas_call(
        functools.partial(_pool_body, tt=tt, tr=tr, pos0=pos0, n_tiles=n_tiles),
        grid=(b, n_tiles),
        in_specs=[pl.BlockSpec((1, tr, C_POOL), lambda bi, ti: (bi, ti, cu)),
                  pl.BlockSpec((1, hp, C_POOL), lambda bi, ti: (bi, 0, 0)),
                  pl.BlockSpec((None, ng, CG, CG), lambda bi, ti: (l, 0, 0, 0)),
                  _wspec(l, 1, C_POOL, lambda bi, ti: (0, 0))],
        out_specs=pl.BlockSpec((1, tr, C_POOL), lambda bi, ti: (bi, ti, 0)),
        out_shape=jax.ShapeDtypeStruct((b, t, C_POOL), BF16),
        scratch_shapes=[pltpu.VMEM((hp + tt, C_POOL), F32)],
        compiler_params=_cparams(("parallel", "arbitrary")),
        name="pool",
    )(proj, hist16, w_pool, pool_scale)


def _tile_rows(m, pref):
    return pref if m % pref == 0 else m


def _layer(x, wts, l, conv_hist, s0, pool_hist, caches, windows, prompt):
    (w_in, conv_w, a_log, dt_bias, gdn_gain, w_pool, pool_scale, w_a, w_b, w_c, w_out, w_gu, w_down,
     g_pre_mix, g_post_mix, g_pre_ffn, g_post_ffn) = wts
    b, t, d = x.shape
    m = b * t
    off, npad = _proj_layout(d)
    x2 = x.reshape(m, d)
    tm = _tile_rows(m, 512)

    proj2 = _norm_matmul(x2, g_pre_mix, w_in, l, _tile_rows(m, 1024), 512)
    proj = proj2.reshape(b, t, npad)

    hist8 = jnp.concatenate([jnp.zeros((b, SUBLANE - (CONV_K - 1), 3 * QK_A), F32), conv_hist], axis=1)
    if prompt:
        out_a, s_new = _gdn(proj, off, conv_w, hist8, s0, a_log, dt_bias, gdn_gain, l, tile=256, chunk=256, hb=4)
    else:
        out_a, s_new = _gdn(proj, off, conv_w, hist8, s0, a_log, dt_bias, gdn_gain, l, tile=256, chunk=8, hb=8)
    qa0 = off["qa"]
    conv_new = jnp.concatenate([conv_hist, proj[:, :, qa0:qa0 + 3 * QK_A]], axis=1)[:, -(CONV_K - 1):]

    kb0, vb0 = off["kb"], off["vb"]
    if prompt:
        out_b = _attn_prompt(proj, off).reshape(m, OUT_B)
        win_new = []
        for gi, (win, _) in enumerate(DIL_GROUPS):
            kg = proj[:, :, kb0 + gi * OUT_B:kb0 + (gi + 1) * OUT_B].reshape(b, t, 1, H_G, HD_B)
            vg = proj[:, :, vb0 + gi * OUT_B:vb0 + (gi + 1) * OUT_B].reshape(b, t, 1, H_G, HD_B)
            win_new.append(jnp.concatenate([kg, vg], axis=2)[:, -win:])
    else:
        out_b, win_new = _attn_sample(proj, off, caches, windows, l)
        out_b = out_b.reshape(m, OUT_B)

    hist16 = jnp.concatenate([jnp.zeros((b, 1, C_POOL), F32), pool_hist], axis=1)
    out_c = _pool(proj, off, hist16, w_pool, pool_scale, l, 0 if prompt else SAMPLE_POS0, 256)
    uc0 = off["uc"]
    pool_new = jnp.concatenate([pool_hist, proj[:, :, uc0:uc0 + C_POOL]], axis=1)[:, -POOL_HIST:]

    merged = _merge(out_a.reshape(m, QK_A), out_b, out_c.reshape(m, C_POOL), w_a, w_b, w_c, proj2, off, l, tm, 512)
    x2 = _mm_norm_res(merged, w_out, x2, g_post_mix, l, tm, 512)
    act = _ffn_up(x2, g_pre_ffn, w_gu, l, tm, 512)
    x2 = _mm_norm_res(act, w_down, x2, g_post_ffn, l, tm, 512)
    return x2.reshape(b, t, d), (win_new[0], win_new[1], win_new[2], s_new, conv_new, pool_new)


def kernel(x_prompt, x_sample, cache_win1, cache_win2, cache_win3, state_gdn, state_conv, state_pool,
           w_in, conv_w, a_log, dt_bias, gdn_gain, w_pool, pool_scale, w_br_a, w_br_b, w_br_c,
           w_out, w_gu, w_down, g_pre_mix, g_post_mix, g_pre_ffn, g_post_ffn):
    depth = w_in.shape[0]
    bp = x_prompt.shape[0]
    d = x_prompt.shape[-1]

    def row(v):
        return v.reshape(depth, 1, v.shape[-1])

    wts = (_prep_w_in(w_in, d), conv_w, a_log, dt_bias, row(gdn_gain),
           w_pool.astype(BF16), row(pool_scale),
           w_br_a.astype(BF16), w_br_b.astype(BF16), w_br_c.astype(BF16),
           w_out.astype(BF16), w_gu.astype(BF16), w_down.astype(BF16),
           row(g_pre_mix), row(g_post_mix), row(g_pre_ffn), row(g_post_ffn))
    yp, ys = x_prompt, x_sample
    caches = (cache_win1, cache_win2, cache_win3)
    windows = tuple(_shift_caches(*caches, x_sample.shape[1]))
    new_p, new_s = [], []
    for l in range(depth):
        yp, st_p = _layer(yp, wts, l,
                          jnp.zeros((bp, CONV_K - 1, 3 * QK_A), F32),
                          jnp.zeros((bp, H_A, DK_A, DK_A), F32),
                          jnp.zeros((bp, POOL_HIST, C_POOL), F32),
                          None, None, True)
        ys, st_s = _layer(ys, wts, l, state_conv[l], state_gdn[l], state_pool[l],
                          caches, windows, False)
        windows = st_s[:3]
        new_p.append(st_p)
        new_s.append(st_s)
    outs_p = [jnp.stack([st[i] for st in new_p], axis=0) for i in range(6)]
    outs_s = [jnp.stack([st[i] for st in new_s], axis=0) for i in range(3, 6)]
    return (yp, ys, *outs_p, *windows, *outs_s)
```

```python
import functools

import jax
import jax.numpy as jnp
from jax import lax
from jax.experimental import pallas as pl
from jax.experimental.pallas import tpu as pltpu

F32 = jnp.float32
BF16 = jnp.bfloat16

H_A = 8
DK_A = 128
CONV_K = 4
DIL_GROUPS = ((128, 1), (512, 4), (2048, 16))
N_GROUPS = len(DIL_GROUPS)
H_G = 4
HD_B = 128
N_BACK = 128
POOL_WINDOWS = (2, 4, 8, 16)
CG = 256
C_POOL = CG * len(POOL_WINDOWS)
POOL_HIST = max(POOL_WINDOWS) - 1
SAMPLE_POS0 = 8192
EPS = 1e-6
L2_EPS = 1e-6

LANE = 128
SUBLANE = 8
VMEM_LIMIT = 48 * 1024 * 1024

QK_A = H_A * DK_A
QKV_B = N_GROUPS * H_G * HD_B
OUT_B = H_G * HD_B
Q_TILE = 128
ATTN_SEGMENTS = 4


def _proj_layout(d_model):
    off = {}
    c = 0
    for name, width in (("qb", QKV_B), ("kb", QKV_B), ("vb", QKV_B),
                        ("ba", H_A), ("aa", H_A), ("pad0", 512 - 2 * H_A),
                        ("uc", C_POOL),
                        ("qa", QK_A), ("ka", QK_A), ("va", QK_A), ("za", QK_A),
                        ("ga", d_model), ("gb", d_model), ("gc", d_model)):
        off[name] = c
        c += width
    total = -(-c // 512) * 512
    return off, total


W_PREP_ROWS = 128


def _prep_w_body(w_ref, o_ref, *, segs):
    pos = 0
    for src, width in segs:
        if src is None:
            o_ref[:, pos:pos + width] = jnp.zeros((o_ref.shape[0], width), BF16)
        else:
            o_ref[:, pos:pos + width] = w_ref[:, src:src + width].astype(BF16)
        pos += width


def _prep_w_in(w, d_model):
    depth, d, n_in = w.shape
    src = {}
    c = 0
    for name, width in (("qa", QK_A), ("ka", QK_A), ("va", QK_A), ("za", QK_A), ("ba", H_A), ("aa", H_A),
                        ("qb", QKV_B), ("kb", QKV_B), ("vb", QKV_B), ("uc", C_POOL),
                        ("ga", d_model), ("gb", d_model), ("gc", d_model)):
        src[name] = (c, width)
        c += width
    assert c == n_in
    off, total = _proj_layout(d_model)
    names = sorted(off, key=off.get)
    segs = []
    for i, name in enumerate(names):
        end = off[names[i + 1]] if i + 1 < len(names) else total
        if name in src:
            s0, width = src[name]
            if segs and segs[-1][0] is not None and segs[-1][0] + segs[-1][1] == s0:
                segs[-1] = (segs[-1][0], segs[-1][1] + width)
            else:
                segs.append((s0, width))
            pad = end - off[name] - width
        else:
            pad = end - off[name]
        if pad:
            segs.append((None, pad))
    tr = W_PREP_ROWS
    return pl.pallas_call(
        functools.partial(_prep_w_body, segs=tuple(segs)),
        grid=(depth, d // tr),
        in_specs=[pl.BlockSpec((None, tr, n_in), lambda l, i: (l, i, 0))],
        out_specs=pl.BlockSpec((None, tr, total), lambda l, i: (l, i, 0)),
        out_shape=jax.ShapeDtypeStruct((depth, d, total), BF16),
        compiler_params=_cparams(("parallel", "parallel")),
        name="prep_w_in",
    )(w)


def _cparams(sem):
    return pltpu.CompilerParams(dimension_semantics=sem, vmem_limit_bytes=VMEM_LIMIT)


def _dot(a, b):
    return jnp.dot(a.astype(BF16), b.astype(BF16), preferred_element_type=F32)


def _dot_nt(a, b):
    return lax.dot_general(a.astype(BF16), b.astype(BF16), (((1,), (1,)), ((), ())),
                           preferred_element_type=F32)


def _dot_tn(a, b):
    return lax.dot_general(a.astype(BF16), b.astype(BF16), (((0,), (0,)), ((), ())),
                           preferred_element_type=F32)


def _split3(x):
    x1 = x.astype(BF16).astype(F32)
    r1 = x - x1
    x2 = r1.astype(BF16).astype(F32)
    x3 = (r1 - x2).astype(BF16).astype(F32)
    return x1, x2, x3


def _rms_scale(y):
    return lax.rsqrt(jnp.mean(y * y, axis=-1, keepdims=True) + EPS)


def _silu(x):
    return x * jax.nn.sigmoid(x)


def _wspec(l, rows, cols, index):
    return pl.BlockSpec((None, rows, cols), lambda *g: (l,) + index(*g))


def _norm_mm_body(x_ref, g_ref, w_ref, o_ref, h_ref):
    @pl.when(pl.program_id(1) == 0)
    def _():
        xf = x_ref[...]
        h_ref[...] = (xf * _rms_scale(xf) * g_ref[...]).astype(BF16)

    o_ref[...] = jnp.dot(h_ref[...], w_ref[...], preferred_element_type=F32)


def _norm_matmul(x, gain, w, l, tm, tn):
    m, d = x.shape
    n = w.shape[-1]
    return pl.pallas_call(
        _norm_mm_body,
        grid=(m // tm, n // tn),
        in_specs=[pl.BlockSpec((tm, d), lambda i, j: (i, 0)),
                  _wspec(l, 1, d, lambda i, j: (0, 0)),
                  _wspec(l, d, tn, lambda i, j: (0, j))],
        out_specs=pl.BlockSpec((tm, tn), lambda i, j: (i, j)),
        out_shape=jax.ShapeDtypeStruct((m, n), F32),
        scratch_shapes=[pltpu.VMEM((tm, d), BF16)],
        compiler_params=_cparams(("parallel", "arbitrary")),
        name="in_proj",
    )(x, gain, w)


def _ffn_up_body(x_ref, g_ref, wg_ref, wu_ref, o_ref, h_ref):
    @pl.when(pl.program_id(1) == 0)
    def _():
        xf = x_ref[...]
        h_ref[...] = (xf * _rms_scale(xf) * g_ref[...]).astype(BF16)

    h = h_ref[...]
    gate = jnp.dot(h, wg_ref[...], preferred_element_type=F32)
    up = jnp.dot(h, wu_ref[...], preferred_element_type=F32)
    o_ref[...] = (_silu(gate) * up).astype(BF16)


def _ffn_up(x, gain, w_gu, l, tm, tn):
    m, d = x.shape
    d_ff = w_gu.shape[-1] // 2
    nj = d_ff // tn
    return pl.pallas_call(
        _ffn_up_body,
        grid=(m // tm, nj),
        in_specs=[pl.BlockSpec((tm, d), lambda i, j: (i, 0)),
                  _wspec(l, 1, d, lambda i, j: (0, 0)),
                  _wspec(l, d, tn, lambda i, j: (0, j)),
                  _wspec(l, d, tn, lambda i, j: (0, j + nj))],
        out_specs=pl.BlockSpec((tm, tn), lambda i, j: (i, j)),
        out_shape=jax.ShapeDtypeStruct((m, d_ff), BF16),
        scratch_shapes=[pltpu.VMEM((tm, d), BF16)],
        compiler_params=_cparams(("parallel", "arbitrary")),
        name="ffn_up",
    )(x, gain, w_gu, w_gu)


def _mm_norm_res_body(a_ref, w_ref, res_ref, g_ref, o_ref, acc_ref, *, nk):
    k = pl.program_id(1)

    @pl.when(k == 0)
    def _():
        acc_ref[...] = jnp.zeros_like(acc_ref)

    acc_ref[...] += jnp.dot(a_ref[...], w_ref[...], preferred_element_type=F32)

    @pl.when(k == nk - 1)
    def _():
        y = acc_ref[...]
        o_ref[...] = res_ref[...] + y * _rms_scale(y) * g_ref[...]


def _mm_norm_res(a, w, res, gain, l, tm, tk):
    m, kdim = a.shape
    d = w.shape[-1]
    nk = kdim // tk
    return pl.pallas_call(
        functools.partial(_mm_norm_res_body, nk=nk),
        grid=(m // tm, nk),
        in_specs=[pl.BlockSpec((tm, tk), lambda i, k: (i, k)),
                  _wspec(l, tk, d, lambda i, k: (k, 0)),
                  pl.BlockSpec((tm, d), lambda i, k: (i, 0)),
                  _wspec(l, 1, d, lambda i, k: (0, 0))],
        out_specs=pl.BlockSpec((tm, d), lambda i, k: (i, 0)),
        out_shape=jax.ShapeDtypeStruct((m, d), F32),
        scratch_shapes=[pltpu.VMEM((tm, d), F32)],
        compiler_params=_cparams(("parallel", "arbitrary")),
        name="mm_norm_res",
    )(a, w, res, gain)


def _merge_body(a_ref, b_ref, c_ref, wa_ref, wb_ref, wc_ref, ga_ref, gb_ref, gc_ref, o_ref):
    m = jax.nn.sigmoid(ga_ref[...]) * jnp.dot(a_ref[...], wa_ref[...], preferred_element_type=F32)
    m = m + jax.nn.sigmoid(gb_ref[...]) * jnp.dot(b_ref[...], wb_ref[...], preferred_element_type=F32)
    m = m + jax.nn.sigmoid(gc_ref[...]) * jnp.dot(c_ref[...], wc_ref[...], preferred_element_type=F32)
    o_ref[...] = m.astype(BF16)


def _merge(out_a, out_b, out_c, w_a, w_b, w_c, proj2d, off, l, tm, tn):
    m = out_a.shape[0]
    d = w_a.shape[-1]
    ja, jb, jc = off["ga"] // tn, off["gb"] // tn, off["gc"] // tn
    return pl.pallas_call(
        _merge_body,
        grid=(m // tm, d // tn),
        in_specs=[pl.BlockSpec((tm, out_a.shape[1]), lambda i, j: (i, 0)),
                  pl.BlockSpec((tm, out_b.shape[1]), lambda i, j: (i, 0)),
                  pl.BlockSpec((tm, out_c.shape[1]), lambda i, j: (i, 0)),
                  _wspec(l, w_a.shape[1], tn, lambda i, j: (0, j)),
                  _wspec(l, w_b.shape[1], tn, lambda i, j: (0, j)),
                  _wspec(l, w_c.shape[1], tn, lambda i, j: (0, j)),
                  pl.BlockSpec((tm, tn), lambda i, j: (i, j + ja)),
                  pl.BlockSpec((tm, tn), lambda i, j: (i, j + jb)),
                  pl.BlockSpec((tm, tn), lambda i, j: (i, j + jc))],
        out_specs=pl.BlockSpec((tm, tn), lambda i, j: (i, j)),
        out_shape=jax.ShapeDtypeStruct((m, d), BF16),
        compiler_params=_cparams(("parallel", "arbitrary")),
        name="merge",
    )(out_a, out_b, out_c, w_a, w_b, w_c, proj2d, proj2d, proj2d)


def _unit_lower_inverse(lower, c):
    row = lax.broadcasted_iota(jnp.int32, (c, c), 0)
    col = lax.broadcasted_iota(jnp.int32, (c, c), 1)

    def same_block(shift):
        return (row >> shift) == (col >> shift)

    eye = jnp.where(row == col, 1.0, 0.0).astype(F32)
    d1 = jnp.where(same_block(3), lower, 0.0)
    d2 = _dot(d1, d1)
    d4 = _dot(d2, d2)
    x = eye - d1
    x = x + _dot(d2, x)
    x = x + _dot(d4, x)
    shift = 3
    while (1 << shift) < c:
        off_diag = jnp.where(jnp.logical_and(same_block(shift + 1), jnp.logical_not(same_block(shift))),
                             lower, 0.0)
        x = x - _dot(_dot(x, off_diag), x)
        shift += 1
    return x


def _gdn_body(alog_ref, dtb_ref, q_ref, k_ref, v_ref, z_ref, ba_ref, cwq_ref, cwk_ref, cwv_ref,
              hq_ref, hk_ref, hv_ref, s0_ref, gain_ref, o_ref, sout_ref,
              s_scr, ext_scr, z_scr, ba_scr, *, tt, tr, t_valid, n_tiles, chunk, hb):
    c = chunk
    h0 = pl.program_id(1) * hb
    it = pl.program_id(2)
    wid = hb * LANE

    @pl.when(it == 0)
    def _():
        s_scr[...] = s0_ref[0]
        ext_scr[0, 0:SUBLANE, :] = hq_ref[0]
        ext_scr[1, 0:SUBLANE, :] = hk_ref[0]
        ext_scr[2, 0:SUBLANE, :] = hv_ref[0]

    if tr < tt:
        for i in range(3):
            ext_scr[i, SUBLANE:, :] = jnp.zeros((tt, wid), F32)
        z_scr[...] = jnp.zeros((tt, wid), F32)
        ba_scr[...] = jnp.zeros((tt, LANE), F32)
        z_scr[0:tr, :] = z_ref[0]
        ba_scr[0:tr, :] = ba_ref[0]
        z_all = z_scr[...]
        ba = ba_scr[...]
    else:
        z_all = z_ref[0]
        ba = ba_ref[0]
    ext_scr[0, SUBLANE:SUBLANE + tr, :] = q_ref[0]
    ext_scr[1, SUBLANE:SUBLANE + tr, :] = k_ref[0]
    ext_scr[2, SUBLANE:SUBLANE + tr, :] = v_ref[0]

    def conv_act(i, cw_ref):
        base = SUBLANE - (CONV_K - 1)
        y = ext_scr[i, base:base + tt, :] * cw_ref[0:1, :]
        for j in range(1, CONV_K):
            y = y + ext_scr[i, base + j:base + j + tt, :] * cw_ref[j:j + 1, :]
        return _silu(y)

    q_all = conv_act(0, cwq_ref)
    k_all = conv_act(1, cwk_ref)
    v_all = conv_act(2, cwv_ref)

    if n_tiles > 1:
        for i in range(3):
            ext_scr[i, 0:SUBLANE, :] = ext_scr[i, tt:tt + SUBLANE, :]

    lane = lax.broadcasted_iota(jnp.int32, (tt, LANE), 1)
    valid = lax.broadcasted_iota(jnp.int32, (tt, 1), 0) < t_valid
    row = lax.broadcasted_iota(jnp.int32, (c, c), 0)
    col = lax.broadcasted_iota(jnp.int32, (c, c), 1)
    incl = row >= col
    strict = row > col
    eye = row == col
    tri = jnp.where(incl, 1.0, 0.0).astype(BF16)

    for hh in range(hb):
        hsl = slice(hh * LANE, (hh + 1) * LANE)
        q = q_all[:, hsl]
        k = k_all[:, hsl]
        v = v_all[:, hsl]
        q = q * lax.rsqrt(jnp.sum(q * q, axis=-1, keepdims=True) + L2_EPS) * (DK_A ** -0.5)
        k = k * lax.rsqrt(jnp.sum(k * k, axis=-1, keepdims=True) + L2_EPS)

        b_raw = jnp.sum(jnp.where(lane == h0 + hh, ba, 0.0), axis=1, keepdims=True)
        a_raw = jnp.sum(jnp.where(lane == h0 + hh + H_A, ba, 0.0), axis=1, keepdims=True)
        beta = jax.nn.sigmoid(b_raw)
        a_log = jnp.full((1, 1), alog_ref[h0 + hh], F32)
        xs = a_raw + dtb_ref[h0 + hh]
        softplus = jnp.maximum(xs, 0.0) + jnp.log1p(jnp.exp(-jnp.abs(xs)))
        g = -jnp.exp(a_log) * softplus

        if t_valid < tt:
            q = jnp.where(valid, q, 0.0)
            k = jnp.where(valid, k, 0.0)
            v = jnp.where(valid, v, 0.0)
            beta = jnp.where(valid, beta, 0.0)
            g = jnp.where(valid, g, 0.0)

        pre = []
        for ci in range(tt // c):
            sl = slice(ci * c, (ci + 1) * c)
            qc, kc, vc, bc, gcol = q[sl], k[sl], v[sl], beta[sl], g[sl]
            g1, g2, g3 = _split3(jnp.broadcast_to(gcol, (c, LANE)))
            lane_c = lax.broadcasted_iota(jnp.int32, (c, LANE), 1)
            g_terms = jnp.where(lane_c == 0, g1, jnp.where(lane_c == 1, g2, g3))
            part = jnp.dot(tri, g_terms.astype(BF16), preferred_element_type=F32)
            gc_col0 = part[:, 0:1] + part[:, 1:2] + part[:, 2:3]
            gc_cb = jnp.broadcast_to(gc_col0, (c, c))
            gc_rb = jnp.sum(jnp.where(eye, gc_cb, 0.0), axis=0, keepdims=True)
            decay = jnp.where(incl, jnp.exp(jnp.where(incl, gc_cb - gc_rb, 0.0)), 0.0)
            kb = kc * bc
            lower = jnp.where(strict, _dot_nt(kb, kc) * decay, 0.0)
            tinv = _unit_lower_inverse(lower, c)
            gc_col = gc_cb[:, 0:1]
            egc = jnp.exp(gc_col)
            u = _dot(tinv, vc * bc)
            w = _dot(tinv, kb * egc)
            a_intra = jnp.where(incl, _dot_nt(qc, kc) * decay, 0.0)
            g_last = gc_cb[c - 1:c, 0:1]
            kd = kc * jnp.exp(g_last - gc_col)
            pre.append((u, w, a_intra, kd, qc * egc, jnp.exp(g_last)))

        s = s_scr[hh]
        outs = []
        for (u, w, a_intra, kd, qe, eg_last) in pre:
            v_new = u - _dot(w, s)
            outs.append(_dot(qe, s) + _dot(a_intra, v_new))
            s = s * eg_last + _dot_tn(kd, v_new)
        s_scr[hh] = s
        o = outs[0] if len(outs) == 1 else jnp.concatenate(outs, axis=0)
        o = o * _rms_scale(o) * gain_ref[...]
        o = o * _silu(z_all[:, hsl])
        o_ref[0, :, hsl] = o[0:tr].astype(o_ref.dtype)

    @pl.when(it == n_tiles - 1)
    def _():
        sout_ref[0] = s_scr[...]


def _gdn(proj, off, conv_w, conv_hist8, s0, a_log, dt_bias, gdn_gain, l, *, tile, chunk, hb):
    b, t, _ = proj.shape
    if t >= tile:
        tt = tr = tile
        n_tiles = t // tile
    else:
        tt, tr, n_tiles = chunk, t, 1
    wid = hb * LANE
    cq, ck, cv, cz = (off[n] // wid for n in ("qa", "ka", "va", "za"))
    cba = off["ba"] // LANE
    nhb = H_A // hb

    def tok(col0):
        return pl.BlockSpec((1, tr, wid), lambda bi, hi, ti: (bi, ti, col0 + hi))

    def per_head(col0, rows):
        return _wspec(l, rows, wid, lambda bi, hi, ti: (0, col0 + hi))

    def hist(col0):
        return pl.BlockSpec((1, SUBLANE, wid), lambda bi, hi, ti: (bi, 0, col0 + hi))

    smem = pl.BlockSpec(memory_space=pltpu.SMEM)
    state_spec = pl.BlockSpec((1, hb, DK_A, DK_A), lambda bi, hi, ti: (bi, hi, 0, 0))
    body = functools.partial(_gdn_body, tt=tt, tr=tr, t_valid=t, n_tiles=n_tiles, chunk=chunk, hb=hb)
    return pl.pallas_call(
        body,
        grid=(b, nhb, n_tiles),
        in_specs=[smem, smem,
                  tok(cq), tok(ck), tok(cv), tok(cz),
                  pl.BlockSpec((1, tr, LANE), lambda bi, hi, ti: (bi, ti, cba)),
                  per_head(0, CONV_K), per_head(nhb, CONV_K), per_head(2 * nhb, CONV_K),
                  hist(0), hist(nhb), hist(2 * nhb),
                  state_spec,
                  _wspec(l, 1, DK_A, lambda bi, hi, ti: (0, 0))],
        out_specs=[pl.BlockSpec((1, tr, wid), lambda bi, hi, ti: (bi, ti, hi)),
                   state_spec],
        out_shape=[jax.ShapeDtypeStruct((b, t, QK_A), BF16),
                   jax.ShapeDtypeStruct((b, H_A, DK_A, DK_A), F32)],
        scratch_shapes=[pltpu.VMEM((hb, DK_A, DK_A), F32),
                        pltpu.VMEM((3, tt + SUBLANE, wid), F32),
                        pltpu.VMEM((tt, wid), F32),
                        pltpu.VMEM((tt, LANE), F32)],
        compiler_params=_cparams(("parallel", "parallel", "arbitrary")),
        name="gdn",
    )(a_log[l], dt_bias[l], proj, proj, proj, proj, proj, conv_w, conv_w, conv_w,
      conv_hist8, conv_hist8, conv_hist8, s0, gdn_gain)


def _attn_body(q0_ref, q1_ref, q2_ref, k0_ref, k1_ref, k2_ref, v0_ref, v1_ref, v2_ref, o_ref,
               kb_scr, vb_scr, *, s_len):
    q_refs = (q0_ref, q1_ref, q2_ref)
    k_refs = (k0_ref, k1_ref, k2_ref)
    v_refs = (v0_ref, v1_ref, v2_ref)
    tq_n = Q_TILE
    scale = HD_B ** -0.5
    for gi in range(N_GROUPS):
        kb_scr[gi] = k_refs[gi][0].astype(BF16)
        vb_scr[gi] = v_refs[gi][0].astype(BF16)

    def tile(i, carry, cap):
        q0 = pl.multiple_of(i * tq_n, tq_n)
        scores = []
        starts = []
        m = None
        for gi, (win, dil) in enumerate(DIL_GROUPS):
            nk = min(cap, win + tq_n)
            k0 = pl.multiple_of(jnp.clip(q0 - win, 0, cap - nk), tq_n)
            q = q_refs[gi][0, pl.ds(q0, tq_n), :]
            s = _dot_nt(q, kb_scr[gi, pl.ds(k0, nk), :]) * scale
            d = ((q0 - k0) + lax.broadcasted_iota(jnp.int32, (tq_n, nk), 0)
                 - lax.broadcasted_iota(jnp.int32, (tq_n, nk), 1))
            ok = jnp.logical_and(jnp.logical_and(d >= 0, d <= win), (d & (dil - 1)) == 0)
            s = jnp.where(ok, s, -jnp.inf)
            mg = jnp.max(s, axis=1, keepdims=True)
            m = mg if m is None else jnp.maximum(m, mg)
            scores.append(s)
            starts.append((k0, nk))
        acc = jnp.zeros((tq_n, HD_B), F32)
        den = jnp.zeros((tq_n, 1), F32)
        for gi in range(N_GROUPS):
            k0, nk = starts[gi]
            p = jnp.exp(scores[gi] - m)
            den = den + jnp.sum(p, axis=1, keepdims=True)
            acc = acc + _dot(p, vb_scr[gi, pl.ds(k0, nk), :])
        o_ref[0, pl.ds(q0, tq_n), :] = (acc / den).astype(o_ref.dtype)
        return carry

    n_tiles = s_len // tq_n
    seg = max(1, n_tiles // ATTN_SEGMENTS)
    for lo in range(0, n_tiles, seg):
        hi = min(lo + seg, n_tiles)
        lax.fori_loop(lo, hi, functools.partial(tile, cap=hi * tq_n), 0)


def _attn_prompt(proj, off):
    b, s_len, _ = proj.shape
    cq, ck, cv = (off[n] // LANE for n in ("qb", "kb", "vb"))

    def spec(col0, gi):
        return pl.BlockSpec((1, s_len, LANE), lambda bi, hi: (bi, 0, col0 + gi * H_G + hi))

    return pl.pallas_call(
        functools.partial(_attn_body, s_len=s_len),
        grid=(b, H_G),
        in_specs=[spec(c0, gi) for c0 in (cq, ck, cv) for gi in range(N_GROUPS)],
        out_specs=pl.BlockSpec((1, s_len, LANE), lambda bi, hi: (bi, 0, hi)),
        out_shape=jax.ShapeDtypeStruct((b, s_len, OUT_B), BF16),
        scratch_shapes=[pltpu.VMEM((N_GROUPS, s_len, HD_B), BF16),
                        pltpu.VMEM((N_GROUPS, s_len, HD_B), BF16)],
        compiler_params=_cparams(("parallel", "parallel")),
        name="attn_prompt",
    )(*([proj] * 9))


SHIFT_UNITS = 128


def _shift_body(a_ref, b_ref, o_ref):
    r = a_ref.shape[0]
    o_ref[0:r - 1] = a_ref[1:r]
    o_ref[r - 1:r] = b_ref[...]


def _shift_cache(c, t_new):
    depth, batch, win = c.shape[:3]
    nu = win // t_new
    r = min(SHIFT_UNITS, nu)
    assert win % t_new == 0 and nu % r == 0
    view = c.reshape(depth, batch, nu, t_new, 2, H_G, HD_B)
    tail = (t_new, 2, H_G, HD_B)
    zeros = (0,) * len(tail)
    out = pl.pallas_call(
        _shift_body,
        grid=(depth, batch, nu // r),
        in_specs=[pl.BlockSpec((None, None, r) + tail, lambda l, b, j: (l, b, j) + zeros),
                  pl.BlockSpec((None, None, 1) + tail,
                               lambda l, b, j: (l, b, jnp.minimum((j + 1) * r, nu - 1)) + zeros)],
        out_specs=pl.BlockSpec((None, None, r) + tail, lambda l, b, j: (l, b, j) + zeros),
        out_shape=jax.ShapeDtypeStruct(view.shape, view.dtype),
        compiler_params=_cparams(("parallel", "parallel", "parallel")),
        name="cache_shift",
    )(view, view)
    return out.reshape(c.shape)


def _attn_sample_body(q_ref, kn_ref, vn_ref, c1_ref, c2_ref, c3_ref, w1_in, w2_in, w3_in,
                      o_ref, w1_ref, w2_ref, w3_ref, *, t_new):
    del w1_in, w2_in, w3_in
    scale = HD_B ** -0.5
    q = q_ref[0]
    kn = kn_ref[0]
    vn = vn_ref[0]
    caches = (c1_ref, c2_ref, c3_ref)
    wins = (w1_ref, w2_ref, w3_ref)
    row_c = lax.broadcasted_iota(jnp.int32, (N_BACK, H_G, 1), 0)

    def heads(x, t, gi):
        c0 = gi * OUT_B
        return jnp.concatenate([x[t:t + 1, c0 + h * HD_B:c0 + (h + 1) * HD_B] for h in range(H_G)], axis=0)

    k_new = [[heads(kn, t, gi) for t in range(t_new)] for gi in range(N_GROUPS)]
    v_new = [[heads(vn, t, gi) for t in range(t_new)] for gi in range(N_GROUPS)]
    for gi in range(N_GROUPS):
        for t in range(t_new):
            wins[gi][0, t, 0] = k_new[gi][t]
            wins[gi][0, t, 1] = v_new[gi][t]

    for t in range(t_new):
        parts = []
        for gi in range(N_GROUPS):
            qv = heads(q, t, gi)
            r = 0 if gi == 0 else t
            kc = caches[gi][0, :, r, 0]
            vc = caches[gi][0, :, r, 1]
            s_c = jnp.sum(kc * qv[None], axis=-1, keepdims=True) * scale
            if gi == 0:
                s_c = jnp.where(row_c >= t, s_c, -jnp.inf)
                news = list(range(t + 1))
            else:
                news = [t]
            s_n = [jnp.sum(k_new[gi][tn] * qv, axis=-1, keepdims=True) * scale for tn in news]
            m = jnp.max(s_c, axis=0)
            for sn in s_n:
                m = jnp.maximum(m, sn)
            p_c = jnp.exp(s_c - m[None])
            l = jnp.sum(p_c, axis=0)
            o = jnp.sum(p_c * vc, axis=0)
            for tn, sn in zip(news, s_n):
                p_n = jnp.exp(sn - m)
                l = l + p_n
                o = o + p_n * v_new[gi][tn]
            parts.append((o, m, l))
        m_all = jnp.maximum(jnp.maximum(parts[0][1], parts[1][1]), parts[2][1])
        num = 0.0
        den = 0.0
        for (o, m, l) in parts:
            e = jnp.exp(m - m_all)
            num = num + e * o
            den = den + e * l
        out = (num / den).astype(o_ref.dtype)
        for h in range(H_G):
            o_ref[0, t:t + 1, h * HD_B:(h + 1) * HD_B] = out[h:h + 1, :]


def _attn_sample(proj, off, caches, windows, l):
    b, t, _ = proj.shape
    depth = caches[0].shape[0]
    assert t <= DIL_GROUPS[1][1] and DIL_GROUPS[0][1] == 1
    views, cspecs = [], []
    for c, (win, dil) in zip(caches, DIL_GROUPS):
        views.append(c.reshape(depth, b, N_BACK, dil, 2, H_G, HD_B))
        rows = min(dil, t)
        cspecs.append(pl.BlockSpec((None, 1, N_BACK, rows, 2, H_G, HD_B),
                                   lambda bi: (l, bi, 0, 0, 0, 0, 0)))
    wspecs = [pl.BlockSpec((None, 1, t, 2, H_G, HD_B),
                           functools.partial(lambda bi, blk: (l, bi, blk, 0, 0, 0), blk=win // t - 1))
              for (win, _) in DIL_GROUPS]

    def pspec(name):
        return pl.BlockSpec((1, t, QKV_B), lambda bi: (bi, 0, off[name] // QKV_B))

    any_spec = pl.BlockSpec(memory_space=pl.ANY)
    res = pl.pallas_call(
        functools.partial(_attn_sample_body, t_new=t),
        grid=(b,),
        in_specs=[pspec("qb"), pspec("kb"), pspec("vb")] + cspecs + [any_spec] * 3,
        out_specs=[pl.BlockSpec((1, t, OUT_B), lambda bi: (bi, 0, 0))] + wspecs,
        out_shape=[jax.ShapeDtypeStruct((b, t, OUT_B), BF16)]
        + [jax.ShapeDtypeStruct(w.shape, w.dtype) for w in windows],
        input_output_aliases={6: 1, 7: 2, 8: 3},
        compiler_params=_cparams(("parallel",)),
        name="attn_sample",
    )(proj, proj, proj, *views, *windows)
    return res[0], tuple(res[1:])


def _pool_body(u_ref, hist_ref, wp_ref, ps_ref, o_ref, ext_scr, *, tt, tr, pos0, n_tiles):
    hp = POOL_HIST + 1
    it = pl.program_id(1)

    @pl.when(it == 0)
    def _():
        ext_scr[0:hp, :] = hist_ref[0]

    if tr < tt:
        ext_scr[hp:, :] = jnp.zeros((tt, C_POOL), F32)
    ext_scr[hp:hp + tr, :] = u_ref[0]

    pos = pos0 + it * tt + lax.broadcasted_iota(jnp.int32, (tt, 1), 0)
    outs = []
    for gi, win in enumerate(POOL_WINDOWS):
        lo, hi = gi * CG, (gi + 1) * CG
        x = ext_scr[hp:hp + tt, lo:hi]
        acc = x
        for back in range(1, win):
            acc = acc + ext_scr[hp - back:hp - back + tt, lo:hi]
        cnt = jnp.minimum(win, pos + 1).astype(F32)
        pooled = acc / cnt - x
        outs.append(jnp.dot(pooled.astype(BF16), wp_ref[gi], preferred_element_type=F32))
    oc = jnp.concatenate(outs, axis=1) * ps_ref[...]
    o_ref[0] = oc[0:tr].astype(o_ref.dtype)

    if n_tiles > 1:
        ext_scr[0:hp, :] = ext_scr[tt:tt + hp, :]


def _pool(proj, off, hist16, w_pool, pool_scale, l, pos0, tile):
    b, t, _ = proj.shape
    hp = POOL_HIST + 1
    if t >= tile:
        tt = tr = tile
        n_tiles = t // tile
    else:
        tt, tr, n_tiles = SUBLANE, t, 1
    cu = off["uc"] // C_POOL
    assert off["uc"] % C_POOL == 0
    ng = len(POOL_WINDOWS)
    return pl.pallas_call(
        functools.partial(_pool_body, tt=tt, tr=tr, pos0=pos0, n_tiles=n_tiles),
        grid=(b, n_tiles),
        in_specs=[pl.BlockSpec((1, tr, C_POOL), lambda bi, ti: (bi, ti, cu)),
                  pl.BlockSpec((1, hp, C_POOL), lambda bi, ti: (bi, 0, 0)),
                  pl.BlockSpec((None, ng, CG, CG), lambda bi, ti: (l, 0, 0, 0)),
                  _wspec(l, 1, C_POOL, lambda bi, ti: (0, 0))],
        out_specs=pl.BlockSpec((1, tr, C_POOL), lambda bi, ti: (bi, ti, 0)),
        out_shape=jax.ShapeDtypeStruct((b, t, C_POOL), BF16),
        scratch_shapes=[pltpu.VMEM((hp + tt, C_POOL), F32)],
        compiler_params=_cparams(("parallel", "arbitrary")),
        name="pool",
    )(proj, hist16, w_pool, pool_scale)


def _tile_rows(m, pref):
    return pref if m % pref == 0 else m


def _layer(x, wts, l, conv_hist, s0, pool_hist, caches, windows, prompt):
    (w_in, conv_w, a_log, dt_bias, gdn_gain, w_pool, pool_scale, w_a, w_b, w_c, w_out, w_gu, w_down,
     g_pre_mix, g_post_mix, g_pre_ffn, g_post_ffn) = wts
    b, t, d = x.shape
    m = b * t
    off, npad = _proj_layout(d)
    x2 = x.reshape(m, d)
    tm = _tile_rows(m, 512)

    proj2 = _norm_matmul(x2, g_pre_mix, w_in, l, _tile_rows(m, 1024), 512)
    proj = proj2.reshape(b, t, npad)

    hist8 = jnp.concatenate([jnp.zeros((b, SUBLANE - (CONV_K - 1), 3 * QK_A), F32), conv_hist], axis=1)
    if prompt:
        out_a, s_new = _gdn(proj, off, conv_w, hist8, s0, a_log, dt_bias, gdn_gain, l, tile=256, chunk=256, hb=4)
    else:
        out_a, s_new = _gdn(proj, off, conv_w, hist8, s0, a_log, dt_bias, gdn_gain, l, tile=256, chunk=8, hb=8)
    qa0 = off["qa"]
    conv_new = jnp.concatenate([conv_hist, proj[:, :, qa0:qa0 + 3 * QK_A]], axis=1)[:, -(CONV_K - 1):]

    kb0, vb0 = off["kb"], off["vb"]
    if prompt:
        out_b = _attn_prompt(proj, off).reshape(m, OUT_B)
        win_new = []
        for gi, (win, _) in enumerate(DIL_GROUPS):
            kg = proj[:, :, kb0 + gi * OUT_B:kb0 + (gi + 1) * OUT_B].reshape(b, t, 1, H_G, HD_B)
            vg = proj[:, :, vb0 + gi * OUT_B:vb0 + (gi + 1) * OUT_B].reshape(b, t, 1, H_G, HD_B)
            win_new.append(jnp.concatenate([kg, vg], axis=2)[:, -win:])
    else:
        out_b, win_new = _attn_sample(proj, off, caches, windows, l)
        out_b = out_b.reshape(m, OUT_B)

    hist16 = jnp.concatenate([jnp.zeros((b, 1, C_POOL), F32), pool_hist], axis=1)
    out_c = _pool(proj, off, hist16, w_pool, pool_scale, l, 0 if prompt else SAMPLE_POS0, 256)
    uc0 = off["uc"]
    pool_new = jnp.concatenate([pool_hist, proj[:, :, uc0:uc0 + C_POOL]], axis=1)[:, -POOL_HIST:]

    merged = _merge(out_a.reshape(m, QK_A), out_b, out_c.reshape(m, C_POOL), w_a, w_b, w_c, proj2, off, l, tm, 512)
    x2 = _mm_norm_res(merged, w_out, x2, g_post_mix, l, tm, 512)
    act = _ffn_up(x2, g_pre_ffn, w_gu, l, tm, 512)
    x2 = _mm_norm_res(act, w_down, x2, g_post_ffn, l, tm, 512)
    return x2.reshape(b, t, d), (win_new[0], win_new[1], win_new[2], s_new, conv_new, pool_new)


def kernel(x_prompt, x_sample, cache_win1, cache_win2, cache_win3, state_gdn, state_conv, state_pool,
           w_in, conv_w, a_log, dt_bias, gdn_gain, w_pool, pool_scale, w_br_a, w_br_b, w_br_c,
           w_out, w_gu, w_down, g_pre_mix, g_post_mix, g_pre_ffn, g_post_ffn):
    depth = w_in.shape[0]
    bp = x_prompt.shape[0]
    d = x_prompt.shape[-1]

    def row(v):
        return v.reshape(depth, 1, v.shape[-1])

    wts = (_prep_w_in(w_in, d), conv_w, a_log, dt_bias, row(gdn_gain),
           w_pool.astype(BF16), row(pool_scale),
           w_br_a.astype(BF16), w_br_b.astype(BF16), w_br_c.astype(BF16),
           w_out.astype(BF16), w_gu.astype(BF16), w_down.astype(BF16),
           row(g_pre_mix), row(g_post_mix), row(g_pre_ffn), row(g_post_ffn))
    yp, ys = x_prompt, x_sample
    caches = (cache_win1, cache_win2, cache_win3)
    windows = tuple(_shift_cache(c, x_sample.shape[1]) for c in caches)
    new_p, new_s = [], []
    for l in range(depth):
        yp, st_p = _layer(yp, wts, l,
                          jnp.zeros((bp, CONV_K - 1, 3 * QK_A), F32),
                          jnp.zeros((bp, H_A, DK_A, DK_A), F32),
                          jnp.zeros((bp, POOL_HIST, C_POOL), F32),
                          None, None, True)
        ys, st_s = _layer(ys, wts, l, state_conv[l], state_gdn[l], state_pool[l],
                          caches, windows, False)
        windows = st_s[:3]
        new_p.append(st_p)
        new_s.append(st_s)
    outs_p = [jnp.stack([st[i] for st in new_p], axis=0) for i in range(6)]
    outs_s = [jnp.stack([st[i] for st in new_s], axis=0) for i in range(3, 6)]
    return (yp, ys, *outs_p, *windows, *outs_s)
```

```python
import functools

import jax
import jax.numpy as jnp
from jax import lax
from jax.experimental import pallas as pl
from jax.experimental.pallas import tpu as pltpu

F32 = jnp.float32
BF16 = jnp.bfloat16

H_A = 8
DK_A = 128
CONV_K = 4
DIL_GROUPS = ((128, 1), (512, 4), (2048, 16))
N_GROUPS = len(DIL_GROUPS)
H_G = 4
HD_B = 128
N_BACK = 128
POOL_WINDOWS = (2, 4, 8, 16)
CG = 256
C_POOL = CG * len(POOL_WINDOWS)
POOL_HIST = max(POOL_WINDOWS) - 1
SAMPLE_POS0 = 8192
EPS = 1e-6
L2_EPS = 1e-6

LANE = 128
SUBLANE = 8
VMEM_LIMIT = 48 * 1024 * 1024

QK_A = H_A * DK_A
QKV_B = N_GROUPS * H_G * HD_B
OUT_B = H_G * HD_B
Q_TILE = 128
ATTN_SEGMENTS = 4


def _proj_layout(d_model):
    off = {}
    c = 0
    for name, width in (("qb", QKV_B), ("kb", QKV_B), ("vb", QKV_B),
                        ("ba", H_A), ("aa", H_A), ("pad0", 512 - 2 * H_A),
                        ("uc", C_POOL),
                        ("qa", QK_A), ("ka", QK_A), ("va", QK_A), ("za", QK_A),
                        ("ga", d_model), ("gb", d_model), ("gc", d_model)):
        off[name] = c
        c += width
    total = -(-c // 512) * 512
    return off, total


W_PREP_ROWS = 128


def _prep_w_body(w_ref, o_ref, *, segs):
    pos = 0
    for src, width in segs:
        if src is None:
            o_ref[:, pos:pos + width] = jnp.zeros((o_ref.shape[0], width), BF16)
        else:
            o_ref[:, pos:pos + width] = w_ref[:, src:src + width].astype(BF16)
        pos += width


def _prep_w_in(w, d_model):
    depth, d, n_in = w.shape
    src = {}
    c = 0
    for name, width in (("qa", QK_A), ("ka", QK_A), ("va", QK_A), ("za", QK_A), ("ba", H_A), ("aa", H_A),
                        ("qb", QKV_B), ("kb", QKV_B), ("vb", QKV_B), ("uc", C_POOL),
                        ("ga", d_model), ("gb", d_model), ("gc", d_model)):
        src[name] = (c, width)
        c += width
    assert c == n_in
    off, total = _proj_layout(d_model)
    names = sorted(off, key=off.get)
    segs = []
    for i, name in enumerate(names):
        end = off[names[i + 1]] if i + 1 < len(names) else total
        if name in src:
            s0, width = src[name]
            if segs and segs[-1][0] is not None and segs[-1][0] + segs[-1][1] == s0:
                segs[-1] = (segs[-1][0], segs[-1][1] + width)
            else:
                segs.append((s0, width))
            pad = end - off[name] - width
        else:
            pad = end - off[name]
        if pad:
            segs.append((None, pad))
    tr = W_PREP_ROWS
    return pl.pallas_call(
        functools.partial(_prep_w_body, segs=tuple(segs)),
        grid=(depth, d // tr),
        in_specs=[pl.BlockSpec((None, tr, n_in), lambda l, i: (l, i, 0))],
        out_specs=pl.BlockSpec((None, tr, total), lambda l, i: (l, i, 0)),
        out_shape=jax.ShapeDtypeStruct((depth, d, total), BF16),
        compiler_params=_cparams(("parallel", "parallel")),
        name="prep_w_in",
    )(w)


def _cparams(sem):
    return pltpu.CompilerParams(dimension_semantics=sem, vmem_limit_bytes=VMEM_LIMIT)


def _dot(a, b):
    return jnp.dot(a.astype(BF16), b.astype(BF16), preferred_element_type=F32)


def _dot_nt(a, b):
    return lax.dot_general(a.astype(BF16), b.astype(BF16), (((1,), (1,)), ((), ())),
                           preferred_element_type=F32)


def _dot_tn(a, b):
    return lax.dot_general(a.astype(BF16), b.astype(BF16), (((0,), (0,)), ((), ())),
                           preferred_element_type=F32)


def _split3(x):
    x1 = x.astype(BF16).astype(F32)
    r1 = x - x1
    x2 = r1.astype(BF16).astype(F32)
    x3 = (r1 - x2).astype(BF16).astype(F32)
    return x1, x2, x3


def _rms_scale(y):
    return lax.rsqrt(jnp.mean(y * y, axis=-1, keepdims=True) + EPS)


def _silu(x):
    return x * jax.nn.sigmoid(x)


def _wspec(l, rows, cols, index):
    return pl.BlockSpec((None, rows, cols), lambda *g: (l,) + index(*g))


def _norm_mm_body(x_ref, g_ref, w_ref, o_ref, h_ref):
    @pl.when(pl.program_id(1) == 0)
    def _():
        xf = x_ref[...]
        h_ref[...] = (xf * _rms_scale(xf) * g_ref[...]).astype(BF16)

    o_ref[...] = jnp.dot(h_ref[...], w_ref[...], preferred_element_type=F32)


def _norm_matmul(x, gain, w, l, tm, tn):
    m, d = x.shape
    n = w.shape[-1]
    return pl.pallas_call(
        _norm_mm_body,
        grid=(m // tm, n // tn),
        in_specs=[pl.BlockSpec((tm, d), lambda i, j: (i, 0)),
                  _wspec(l, 1, d, lambda i, j: (0, 0)),
                  _wspec(l, d, tn, lambda i, j: (0, j))],
        out_specs=pl.BlockSpec((tm, tn), lambda i, j: (i, j)),
        out_shape=jax.ShapeDtypeStruct((m, n), F32),
        scratch_shapes=[pltpu.VMEM((tm, d), BF16)],
        compiler_params=_cparams(("parallel", "arbitrary")),
        name="in_proj",
    )(x, gain, w)


def _ffn_up_body(x_ref, g_ref, wg_ref, wu_ref, o_ref, h_ref):
    @pl.when(pl.program_id(1) == 0)
    def _():
        xf = x_ref[...]
        h_ref[...] = (xf * _rms_scale(xf) * g_ref[...]).astype(BF16)

    h = h_ref[...]
    gate = jnp.dot(h, wg_ref[...], preferred_element_type=F32)
    up = jnp.dot(h, wu_ref[...], preferred_element_type=F32)
    o_ref[...] = (_silu(gate) * up).astype(BF16)


def _ffn_up(x, gain, w_gu, l, tm, tn):
    m, d = x.shape
    d_ff = w_gu.shape[-1] // 2
    nj = d_ff // tn
    return pl.pallas_call(
        _ffn_up_body,
        grid=(m // tm, nj),
        in_specs=[pl.BlockSpec((tm, d), lambda i, j: (i, 0)),
                  _wspec(l, 1, d, lambda i, j: (0, 0)),
                  _wspec(l, d, tn, lambda i, j: (0, j)),
                  _wspec(l, d, tn, lambda i, j: (0, j + nj))],
        out_specs=pl.BlockSpec((tm, tn), lambda i, j: (i, j)),
        out_shape=jax.ShapeDtypeStruct((m, d_ff), BF16),
        scratch_shapes=[pltpu.VMEM((tm, d), BF16)],
        compiler_params=_cparams(("parallel", "arbitrary")),
        name="ffn_up",
    )(x, gain, w_gu, w_gu)


def _mm_norm_res_body(a_ref, w_ref, res_ref, g_ref, o_ref, acc_ref, *, nk):
    k = pl.program_id(1)

    @pl.when(k == 0)
    def _():
        acc_ref[...] = jnp.zeros_like(acc_ref)

    acc_ref[...] += jnp.dot(a_ref[...], w_ref[...], preferred_element_type=F32)

    @pl.when(k == nk - 1)
    def _():
        y = acc_ref[...]
        o_ref[...] = res_ref[...] + y * _rms_scale(y) * g_ref[...]


def _mm_norm_res_single_body(a_ref, w_ref, res_ref, g_ref, o_ref):
    y = jnp.dot(a_ref[...], w_ref[...], preferred_element_type=F32)
    o_ref[...] = res_ref[...] + y * _rms_scale(y) * g_ref[...]


def _mm_norm_res(a, w, res, gain, l, tm, tk):
    m, kdim = a.shape
    d = w.shape[-1]
    nk = kdim // tk
    assert nk * tk == kdim
    if nk == 1:
        return pl.pallas_call(
            _mm_norm_res_single_body,
            grid=(m // tm,),
            in_specs=[pl.BlockSpec((tm, kdim), lambda i: (i, 0)),
                      _wspec(l, kdim, d, lambda i: (0, 0)),
                      pl.BlockSpec((tm, d), lambda i: (i, 0)),
                      _wspec(l, 1, d, lambda i: (0, 0))],
            out_specs=pl.BlockSpec((tm, d), lambda i: (i, 0)),
            out_shape=jax.ShapeDtypeStruct((m, d), F32),
            compiler_params=_cparams(("parallel",)),
            name="mm_norm_res",
        )(a, w, res, gain)
    return pl.pallas_call(
        functools.partial(_mm_norm_res_body, nk=nk),
        grid=(m // tm, nk),
        in_specs=[pl.BlockSpec((tm, tk), lambda i, k: (i, k)),
                  _wspec(l, tk, d, lambda i, k: (k, 0)),
                  pl.BlockSpec((tm, d), lambda i, k: (i, 0)),
                  _wspec(l, 1, d, lambda i, k: (0, 0))],
        out_specs=pl.BlockSpec((tm, d), lambda i, k: (i, 0)),
        out_shape=jax.ShapeDtypeStruct((m, d), F32),
        scratch_shapes=[pltpu.VMEM((tm, d), F32)],
        compiler_params=_cparams(("parallel", "arbitrary")),
        name="mm_norm_res",
    )(a, w, res, gain)


def _merge_body(a_ref, b_ref, c_ref, wa_ref, wb_ref, wc_ref, ga_ref, gb_ref, gc_ref, o_ref):
    m = jax.nn.sigmoid(ga_ref[...]) * jnp.dot(a_ref[...], wa_ref[...], preferred_element_type=F32)
    m = m + jax.nn.sigmoid(gb_ref[...]) * jnp.dot(b_ref[...], wb_ref[...], preferred_element_type=F32)
    m = m + jax.nn.sigmoid(gc_ref[...]) * jnp.dot(c_ref[...], wc_ref[...], preferred_element_type=F32)
    o_ref[...] = m.astype(BF16)


def _merge(out_a, out_b, out_c, w_a, w_b, w_c, proj2d, off, l, tm, tn):
    m = out_a.shape[0]
    d = w_a.shape[-1]
    ja, jb, jc = off["ga"] // tn, off["gb"] // tn, off["gc"] // tn
    return pl.pallas_call(
        _merge_body,
        grid=(m // tm, d // tn),
        in_specs=[pl.BlockSpec((tm, out_a.shape[1]), lambda i, j: (i, 0)),
                  pl.BlockSpec((tm, out_b.shape[1]), lambda i, j: (i, 0)),
                  pl.BlockSpec((tm, out_c.shape[1]), lambda i, j: (i, 0)),
                  _wspec(l, w_a.shape[1], tn, lambda i, j: (0, j)),
                  _wspec(l, w_b.shape[1], tn, lambda i, j: (0, j)),
                  _wspec(l, w_c.shape[1], tn, lambda i, j: (0, j)),
                  pl.BlockSpec((tm, tn), lambda i, j: (i, j + ja)),
                  pl.BlockSpec((tm, tn), lambda i, j: (i, j + jb)),
                  pl.BlockSpec((tm, tn), lambda i, j: (i, j + jc))],
        out_specs=pl.BlockSpec((tm, tn), lambda i, j: (i, j)),
        out_shape=jax.ShapeDtypeStruct((m, d), BF16),
        compiler_params=_cparams(("parallel", "arbitrary")),
        name="merge",
    )(out_a, out_b, out_c, w_a, w_b, w_c, proj2d, proj2d, proj2d)


def _unit_lower_inverse(lower, c):
    n = lower.shape[0]
    row = lax.broadcasted_iota(jnp.int32, (n, n), 0)
    col = lax.broadcasted_iota(jnp.int32, (n, n), 1)

    def same_block(shift):
        return (row >> shift) == (col >> shift)

    eye = jnp.where(row == col, 1.0, 0.0).astype(F32)
    d1 = jnp.where(same_block(3), lower, 0.0)
    d2 = _dot(d1, d1)
    d4 = _dot(d2, d2)
    x = eye - d1
    x = x + _dot(d2, x)
    x = x + _dot(d4, x)
    shift = 3
    while (1 << shift) < c:
        off_diag = jnp.where(jnp.logical_and(same_block(shift + 1), jnp.logical_not(same_block(shift))),
                             lower, 0.0)
        x = x - _dot(_dot(x, off_diag), x)
        shift += 1
    return x


def _gdn_body(alog_ref, dtb_ref, q_ref, k_ref, v_ref, z_ref, ba_ref, cwq_ref, cwk_ref, cwv_ref,
              hq_ref, hk_ref, hv_ref, s0_ref, gain_ref, o_ref, sout_ref,
              s_scr, ext_scr, z_scr, ba_scr, *, tt, tr, t_valid, n_tiles, chunk, hb):
    c = chunk
    h0 = pl.program_id(1) * hb
    it = pl.program_id(2)
    wid = hb * LANE

    @pl.when(it == 0)
    def _():
        s_scr[...] = s0_ref[0]
        ext_scr[0, 0:SUBLANE, :] = hq_ref[0]
        ext_scr[1, 0:SUBLANE, :] = hk_ref[0]
        ext_scr[2, 0:SUBLANE, :] = hv_ref[0]

    if tr < tt:
        for i in range(3):
            ext_scr[i, SUBLANE:, :] = jnp.zeros((tt, wid), F32)
        z_scr[...] = jnp.zeros((tt, wid), F32)
        ba_scr[...] = jnp.zeros((tt, LANE), F32)
        z_scr[0:tr, :] = z_ref[0]
        ba_scr[0:tr, :] = ba_ref[0]
        z_all = z_scr[...]
        ba = ba_scr[...]
    else:
        z_all = z_ref[0]
        ba = ba_ref[0]
    ext_scr[0, SUBLANE:SUBLANE + tr, :] = q_ref[0]
    ext_scr[1, SUBLANE:SUBLANE + tr, :] = k_ref[0]
    ext_scr[2, SUBLANE:SUBLANE + tr, :] = v_ref[0]

    def conv_act(i, cw_ref):
        base = SUBLANE - (CONV_K - 1)
        y = ext_scr[i, base:base + tt, :] * cw_ref[0:1, :]
        for j in range(1, CONV_K):
            y = y + ext_scr[i, base + j:base + j + tt, :] * cw_ref[j:j + 1, :]
        return _silu(y)

    q_all = conv_act(0, cwq_ref)
    k_all = conv_act(1, cwk_ref)
    v_all = conv_act(2, cwv_ref)

    if n_tiles > 1:
        for i in range(3):
            ext_scr[i, 0:SUBLANE, :] = ext_scr[i, tt:tt + SUBLANE, :]

    lane = lax.broadcasted_iota(jnp.int32, (tt, LANE), 1)
    valid = lax.broadcasted_iota(jnp.int32, (tt, 1), 0) < t_valid
    row = lax.broadcasted_iota(jnp.int32, (c, c), 0)
    col = lax.broadcasted_iota(jnp.int32, (c, c), 1)
    incl = row >= col
    strict = row > col
    eye = row == col
    tri = jnp.where(incl, 1.0, 0.0).astype(BF16)

    for hh in range(hb):
        hsl = slice(hh * LANE, (hh + 1) * LANE)
        q = q_all[:, hsl]
        k = k_all[:, hsl]
        v = v_all[:, hsl]
        q = q * lax.rsqrt(jnp.sum(q * q, axis=-1, keepdims=True) + L2_EPS) * (DK_A ** -0.5)
        k = k * lax.rsqrt(jnp.sum(k * k, axis=-1, keepdims=True) + L2_EPS)

        b_raw = jnp.sum(jnp.where(lane == h0 + hh, ba, 0.0), axis=1, keepdims=True)
        a_raw = jnp.sum(jnp.where(lane == h0 + hh + H_A, ba, 0.0), axis=1, keepdims=True)
        beta = jax.nn.sigmoid(b_raw)
        a_log = jnp.full((1, 1), alog_ref[h0 + hh], F32)
        xs = a_raw + dtb_ref[h0 + hh]
        softplus = jnp.maximum(xs, 0.0) + jnp.log1p(jnp.exp(-jnp.abs(xs)))
        g = -jnp.exp(a_log) * softplus

        if t_valid < tt:
            q = jnp.where(valid, q, 0.0)
            k = jnp.where(valid, k, 0.0)
            v = jnp.where(valid, v, 0.0)
            beta = jnp.where(valid, beta, 0.0)
            g = jnp.where(valid, g, 0.0)

        pre = []
        for ci in range(tt // c):
            sl = slice(ci * c, (ci + 1) * c)
            qc, kc, vc, bc, gcol = q[sl], k[sl], v[sl], beta[sl], g[sl]
            g1, g2, g3 = _split3(jnp.broadcast_to(gcol, (c, LANE)))
            lane_c = lax.broadcasted_iota(jnp.int32, (c, LANE), 1)
            g_terms = jnp.where(lane_c == 0, g1, jnp.where(lane_c == 1, g2, g3))
            part = jnp.dot(tri, g_terms.astype(BF16), preferred_element_type=F32)
            gc_col0 = part[:, 0:1] + part[:, 1:2] + part[:, 2:3]
            gc_cb = jnp.broadcast_to(gc_col0, (c, c))
            gc_rb = jnp.sum(jnp.where(eye, gc_cb, 0.0), axis=0, keepdims=True)
            decay = jnp.where(incl, jnp.exp(jnp.where(incl, gc_cb - gc_rb, 0.0)), 0.0)
            kb = kc * bc
            kq = _dot_nt(jnp.concatenate([kb, qc], axis=0), kc)
            lower = jnp.where(strict, kq[0:c] * decay, 0.0)
            a_intra = jnp.where(incl, kq[c:2 * c] * decay, 0.0)
            tinv = _unit_lower_inverse(lower, c)
            gc_col = gc_cb[:, 0:1]
            egc = jnp.exp(gc_col)
            uw = _dot(tinv, jnp.concatenate([vc * bc, kb * egc], axis=1))
            g_last = gc_cb[c - 1:c, 0:1]
            kd = kc * jnp.exp(g_last - gc_col)
            w_qe = jnp.concatenate([uw[:, DK_A:2 * DK_A], qc * egc], axis=0)
            pre.append((uw[:, 0:DK_A], w_qe, a_intra, kd, jnp.exp(g_last)))

        s = s_scr[hh]
        outs = []
        for (u, w_qe, a_intra, kd, eg_last) in pre:
            ws_qs = _dot(w_qe, s)
            v_new = u - ws_qs[0:c]
            outs.append(ws_qs[c:2 * c] + _dot(a_intra, v_new))
            s = s * eg_last + _dot_tn(kd, v_new)
        s_scr[hh] = s
        o = outs[0] if len(outs) == 1 else jnp.concatenate(outs, axis=0)
        o = o * _rms_scale(o) * gain_ref[...]
        o = o * _silu(z_all[:, hsl])
        o_ref[0, :, hsl] = o[0:tr].astype(o_ref.dtype)

    @pl.when(it == n_tiles - 1)
    def _():
        sout_ref[0] = s_scr[...]


def _gdn(proj, off, conv_w, conv_hist8, s0, a_log, dt_bias, gdn_gain, l, *, tile, chunk, hb):
    b, t, _ = proj.shape
    if t >= tile:
        tt = tr = tile
        n_tiles = t // tile
    else:
        tt, tr, n_tiles = chunk, t, 1
    wid = hb * LANE
    cq, ck, cv, cz = (off[n] // wid for n in ("qa", "ka", "va", "za"))
    cba = off["ba"] // LANE
    nhb = H_A // hb

    def tok(col0):
        return pl.BlockSpec((1, tr, wid), lambda bi, hi, ti: (bi, ti, col0 + hi))

    def per_head(col0, rows):
        return _wspec(l, rows, wid, lambda bi, hi, ti: (0, col0 + hi))

    def hist(col0):
        return pl.BlockSpec((1, SUBLANE, wid), lambda bi, hi, ti: (bi, 0, col0 + hi))

    smem = pl.BlockSpec(memory_space=pltpu.SMEM)
    state_spec = pl.BlockSpec((1, hb, DK_A, DK_A), lambda bi, hi, ti: (bi, hi, 0, 0))
    body = functools.partial(_gdn_body, tt=tt, tr=tr, t_valid=t, n_tiles=n_tiles, chunk=chunk, hb=hb)
    return pl.pallas_call(
        body,
        grid=(b, nhb, n_tiles),
        in_specs=[smem, smem,
                  tok(cq), tok(ck), tok(cv), tok(cz),
                  pl.BlockSpec((1, tr, LANE), lambda bi, hi, ti: (bi, ti, cba)),
                  per_head(0, CONV_K), per_head(nhb, CONV_K), per_head(2 * nhb, CONV_K),
                  hist(0), hist(nhb), hist(2 * nhb),
                  state_spec,
                  _wspec(l, 1, DK_A, lambda bi, hi, ti: (0, 0))],
        out_specs=[pl.BlockSpec((1, tr, wid), lambda bi, hi, ti: (bi, ti, hi)),
                   state_spec],
        out_shape=[jax.ShapeDtypeStruct((b, t, QK_A), BF16),
                   jax.ShapeDtypeStruct((b, H_A, DK_A, DK_A), F32)],
        scratch_shapes=[pltpu.VMEM((hb, DK_A, DK_A), F32),
                        pltpu.VMEM((3, tt + SUBLANE, wid), F32),
                        pltpu.VMEM((tt, wid), F32),
                        pltpu.VMEM((tt, LANE), F32)],
        compiler_params=_cparams(("parallel", "parallel", "arbitrary")),
        name="gdn",
    )(a_log[l], dt_bias[l], proj, proj, proj, proj, proj, conv_w, conv_w, conv_w,
      conv_hist8, conv_hist8, conv_hist8, s0, gdn_gain)


def _gdn_sample_body(alog_ref, dtb_ref, q_ref, k_ref, v_ref, z_ref, ba_ref, cwq_ref, cwk_ref, cwv_ref,
                     hq_ref, hk_ref, hv_ref, s0_ref, gain_ref, o_ref, sout_ref, ext_scr, z_scr, ba_scr, *, tr):
    tp = SUBLANE
    n = H_A * tp
    for i, (x_ref, h_ref) in enumerate(((q_ref, hq_ref), (k_ref, hk_ref), (v_ref, hv_ref))):
        ext_scr[i, 0:SUBLANE, :] = h_ref[0]
        ext_scr[i, SUBLANE:, :] = jnp.zeros((tp, QK_A), F32)
        ext_scr[i, SUBLANE:SUBLANE + tr, :] = x_ref[0]

    def conv_act(i, cw_ref):
        base = SUBLANE - (CONV_K - 1)
        y = ext_scr[i, base:base + tp, :] * cw_ref[0:1, :]
        for j in range(1, CONV_K):
            y = y + ext_scr[i, base + j:base + j + tp, :] * cw_ref[j:j + 1, :]
        return _silu(y)

    def pack(x):
        return jnp.concatenate([x[:, h * LANE:(h + 1) * LANE] for h in range(H_A)], axis=0)

    q = pack(conv_act(0, cwq_ref))
    k = pack(conv_act(1, cwk_ref))
    v = pack(conv_act(2, cwv_ref))
    q = q * lax.rsqrt(jnp.sum(q * q, axis=-1, keepdims=True) + L2_EPS) * (DK_A ** -0.5)
    k = k * lax.rsqrt(jnp.sum(k * k, axis=-1, keepdims=True) + L2_EPS)

    ba_scr[...] = jnp.zeros((tp, LANE), F32)
    ba_scr[0:tr, :] = ba_ref[0]
    ba = ba_scr[...]
    b_raw = jnp.concatenate([ba[:, h:h + 1] for h in range(H_A)], axis=0)
    a_raw = jnp.concatenate([ba[:, H_A + h:H_A + h + 1] for h in range(H_A)], axis=0)
    a_log = jnp.concatenate([jnp.full((tp, 1), alog_ref[h], F32) for h in range(H_A)], axis=0)
    dtb = jnp.concatenate([jnp.full((tp, 1), dtb_ref[h], F32) for h in range(H_A)], axis=0)
    beta = jax.nn.sigmoid(b_raw)
    xs = a_raw + dtb
    g = -jnp.exp(a_log) * (jnp.maximum(xs, 0.0) + jnp.log1p(jnp.exp(-jnp.abs(xs))))

    valid = (lax.broadcasted_iota(jnp.int32, (n, 1), 0) & (tp - 1)) < tr
    q = jnp.where(valid, q, 0.0)
    k = jnp.where(valid, k, 0.0)
    v = jnp.where(valid, v, 0.0)
    beta = jnp.where(valid, beta, 0.0)
    g = jnp.where(valid, g, 0.0)

    row = lax.broadcasted_iota(jnp.int32, (n, n), 0)
    col = lax.broadcasted_iota(jnp.int32, (n, n), 1)
    same_head = (row >> 3) == (col >> 3)
    incl = jnp.logical_and(same_head, row >= col)
    strict = jnp.logical_and(same_head, row > col)
    eye = row == col
    sum_mat = jnp.concatenate([jnp.where(incl, 1.0, 0.0), jnp.where(same_head, 1.0, 0.0)], axis=0).astype(BF16)
    g1, g2, g3 = _split3(jnp.broadcast_to(g, (n, LANE)))
    lane_c = lax.broadcasted_iota(jnp.int32, (n, LANE), 1)
    g_terms = jnp.where(lane_c == 0, g1, jnp.where(lane_c == 1, g2, g3))
    part = jnp.dot(sum_mat, g_terms.astype(BF16), preferred_element_type=F32)
    sums = part[:, 0:1] + part[:, 1:2] + part[:, 2:3]
    gc_col = sums[0:n]
    g_last = sums[n:2 * n]
    gc_cb = jnp.broadcast_to(gc_col, (n, n))
    gc_rb = jnp.sum(jnp.where(eye, gc_cb, 0.0), axis=0, keepdims=True)
    decay = jnp.where(incl, jnp.exp(jnp.where(incl, gc_cb - gc_rb, 0.0)), 0.0)
    kb = k * beta
    kq = _dot_nt(jnp.concatenate([kb, q], axis=0), k)
    lower = jnp.where(strict, kq[0:n] * decay, 0.0)
    a_intra = jnp.where(incl, kq[n:2 * n] * decay, 0.0)
    tinv = _unit_lower_inverse(lower, tp)
    egc = jnp.exp(gc_col)
    uw = _dot(tinv, jnp.concatenate([v * beta, kb * egc], axis=1))
    kd = k * jnp.exp(g_last - gc_col)

    row_e = lax.broadcasted_iota(jnp.int32, (n, QK_A), 0)
    col_e = lax.broadcasted_iota(jnp.int32, (n, QK_A), 1)
    own = (row_e >> 3) == (col_e >> 7)

    def expand(x):
        return jnp.where(own, jnp.concatenate([x] * H_A, axis=1), 0.0)

    s = s0_ref[0].reshape(H_A * DK_A, DK_A)
    ws_qs = _dot(jnp.concatenate([expand(uw[:, DK_A:2 * DK_A]), expand(q * egc)], axis=0), s)
    v_new = uw[:, 0:DK_A] - ws_qs[0:n]
    o = ws_qs[n:2 * n] + _dot(a_intra, v_new)
    upd = _dot_tn(expand(kd), v_new)
    for h in range(H_A):
        eg = jnp.exp(g_last[h * tp:h * tp + 1, :])
        sout_ref[0, h] = s[h * DK_A:(h + 1) * DK_A] * eg + upd[h * DK_A:(h + 1) * DK_A]

    o = o * _rms_scale(o) * gain_ref[...]
    z_scr[...] = jnp.zeros((tp, QK_A), F32)
    z_scr[0:tr, :] = z_ref[0]
    o = o * _silu(pack(z_scr[...]))
    o_tok = jnp.concatenate([o[h * tp:(h + 1) * tp] for h in range(H_A)], axis=1)
    o_ref[0] = o_tok[0:tr].astype(o_ref.dtype)


def _gdn_sample(proj, off, conv_w, conv_hist8, s0, a_log, dt_bias, gdn_gain, l):
    b, t, _ = proj.shape
    assert t <= SUBLANE
    cq, ck, cv, cz = (off[n] // QK_A for n in ("qa", "ka", "va", "za"))
    cba = off["ba"] // LANE

    def tok(col0):
        return pl.BlockSpec((1, t, QK_A), lambda bi: (bi, 0, col0))

    def conv(col0):
        return _wspec(l, CONV_K, QK_A, lambda bi: (0, col0))

    def hist(col0):
        return pl.BlockSpec((1, SUBLANE, QK_A), lambda bi: (bi, 0, col0))

    smem = pl.BlockSpec(memory_space=pltpu.SMEM)
    state_spec = pl.BlockSpec((1, H_A, DK_A, DK_A), lambda bi: (bi, 0, 0, 0))
    return pl.pallas_call(
        functools.partial(_gdn_sample_body, tr=t),
        grid=(b,),
        in_specs=[smem, smem, tok(cq), tok(ck), tok(cv), tok(cz),
                  pl.BlockSpec((1, t, LANE), lambda bi: (bi, 0, cba)),
                  conv(0), conv(1), conv(2), hist(0), hist(1), hist(2),
                  state_spec, _wspec(l, 1, DK_A, lambda bi: (0, 0))],
        out_specs=[pl.BlockSpec((1, t, QK_A), lambda bi: (bi, 0, 0)), state_spec],
        out_shape=[jax.ShapeDtypeStruct((b, t, QK_A), BF16),
                   jax.ShapeDtypeStruct((b, H_A, DK_A, DK_A), F32)],
        scratch_shapes=[pltpu.VMEM((3, 2 * SUBLANE, QK_A), F32),
                        pltpu.VMEM((SUBLANE, QK_A), F32),
                        pltpu.VMEM((SUBLANE, LANE), F32)],
        compiler_params=_cparams(("parallel",)),
        name="gdn_sample",
    )(a_log[l], dt_bias[l], proj, proj, proj, proj, proj, conv_w, conv_w, conv_w,
      conv_hist8, conv_hist8, conv_hist8, s0, gdn_gain)


def _attn_body(q0_ref, q1_ref, q2_ref, k0_ref, k1_ref, k2_ref, v0_ref, v1_ref, v2_ref, o_ref,
               kb_scr, vb_scr, *, s_len):
    q_refs = (q0_ref, q1_ref, q2_ref)
    k_refs = (k0_ref, k1_ref, k2_ref)
    v_refs = (v0_ref, v1_ref, v2_ref)
    tq_n = Q_TILE
    scale = HD_B ** -0.5
    for gi in range(N_GROUPS):
        kb_scr[gi] = k_refs[gi][0].astype(BF16)
        vb_scr[gi] = v_refs[gi][0].astype(BF16)

    def tile(i, carry, cap):
        q0 = pl.multiple_of(i * tq_n, tq_n)
        scores = []
        starts = []
        m = None
        for gi, (win, dil) in enumerate(DIL_GROUPS):
            nk = min(cap, win + tq_n)
            k0 = pl.multiple_of(jnp.clip(q0 - win, 0, cap - nk), tq_n)
            q = q_refs[gi][0, pl.ds(q0, tq_n), :]
            s = _dot_nt(q, kb_scr[gi, pl.ds(k0, nk), :]) * scale
            d = ((q0 - k0) + lax.broadcasted_iota(jnp.int32, (tq_n, nk), 0)
                 - lax.broadcasted_iota(jnp.int32, (tq_n, nk), 1))
            ok = jnp.logical_and(jnp.logical_and(d >= 0, d <= win), (d & (dil - 1)) == 0)
            s = jnp.where(ok, s, -jnp.inf)
            mg = jnp.max(s, axis=1, keepdims=True)
            m = mg if m is None else jnp.maximum(m, mg)
            scores.append(s)
            starts.append((k0, nk))
        acc = jnp.zeros((tq_n, HD_B), F32)
        den = jnp.zeros((tq_n, 1), F32)
        for gi in range(N_GROUPS):
            k0, nk = starts[gi]
            p = jnp.exp(scores[gi] - m)
            den = den + jnp.sum(p, axis=1, keepdims=True)
            acc = acc + _dot(p, vb_scr[gi, pl.ds(k0, nk), :])
        o_ref[0, pl.ds(q0, tq_n), :] = (acc / den).astype(o_ref.dtype)
        return carry

    n_tiles = s_len // tq_n
    seg = max(1, n_tiles // ATTN_SEGMENTS)
    for lo in range(0, n_tiles, seg):
        hi = min(lo + seg, n_tiles)
        lax.fori_loop(lo, hi, functools.partial(tile, cap=hi * tq_n), 0)


def _attn_prompt(proj, off):
    b, s_len, _ = proj.shape
    cq, ck, cv = (off[n] // LANE for n in ("qb", "kb", "vb"))

    def spec(col0, gi):
        return pl.BlockSpec((1, s_len, LANE), lambda bi, hi: (bi, 0, col0 + gi * H_G + hi))

    return pl.pallas_call(
        functools.partial(_attn_body, s_len=s_len),
        grid=(b, H_G),
        in_specs=[spec(c0, gi) for c0 in (cq, ck, cv) for gi in range(N_GROUPS)],
        out_specs=pl.BlockSpec((1, s_len, LANE), lambda bi, hi: (bi, 0, hi)),
        out_shape=jax.ShapeDtypeStruct((b, s_len, OUT_B), BF16),
        scratch_shapes=[pltpu.VMEM((N_GROUPS, s_len, HD_B), BF16),
                        pltpu.VMEM((N_GROUPS, s_len, HD_B), BF16)],
        compiler_params=_cparams(("parallel", "parallel")),
        name="attn_prompt",
    )(*([proj] * 9))


SHIFT_UNITS = 128


def _shift_body(a_ref, b_ref, o_ref):
    r = a_ref.shape[0]
    o_ref[0:r - 1] = a_ref[1:r]
    o_ref[r - 1:r] = b_ref[...]


def _shift_cache(c, t_new):
    depth, batch, win = c.shape[:3]
    nu = win // t_new
    r = min(SHIFT_UNITS, nu)
    assert win % t_new == 0 and nu % r == 0
    view = c.reshape(depth, batch, nu, t_new, 2, H_G, HD_B)
    tail = (t_new, 2, H_G, HD_B)
    zeros = (0,) * len(tail)
    out = pl.pallas_call(
        _shift_body,
        grid=(depth, batch, nu // r),
        in_specs=[pl.BlockSpec((None, None, r) + tail, lambda l, b, j: (l, b, j) + zeros),
                  pl.BlockSpec((None, None, 1) + tail,
                               lambda l, b, j: (l, b, jnp.minimum((j + 1) * r, nu - 1)) + zeros)],
        out_specs=pl.BlockSpec((None, None, r) + tail, lambda l, b, j: (l, b, j) + zeros),
        out_shape=jax.ShapeDtypeStruct(view.shape, view.dtype),
        compiler_params=_cparams(("parallel", "parallel", "parallel")),
        name="cache_shift",
    )(view, view)
    return out.reshape(c.shape)


def _attn_sample_body(q_ref, kn_ref, vn_ref, c1_ref, c2_ref, c3_ref, w1_in, w2_in, w3_in,
                      o_ref, w1_ref, w2_ref, w3_ref, *, t_new):
    del w1_in, w2_in, w3_in
    scale = HD_B ** -0.5
    q = q_ref[0]
    kn = kn_ref[0]
    vn = vn_ref[0]
    caches = (c1_ref, c2_ref, c3_ref)
    wins = (w1_ref, w2_ref, w3_ref)
    row_c = lax.broadcasted_iota(jnp.int32, (N_BACK, H_G, 1), 0)

    def heads(x, t, gi):
        c0 = gi * OUT_B
        return jnp.concatenate([x[t:t + 1, c0 + h * HD_B:c0 + (h + 1) * HD_B] for h in range(H_G)], axis=0)

    k_new = [[heads(kn, t, gi) for t in range(t_new)] for gi in range(N_GROUPS)]
    v_new = [[heads(vn, t, gi) for t in range(t_new)] for gi in range(N_GROUPS)]
    for gi in range(N_GROUPS):
        for t in range(t_new):
            wins[gi][0, t, 0] = k_new[gi][t]
            wins[gi][0, t, 1] = v_new[gi][t]

    for t in range(t_new):
        parts = []
        for gi in range(N_GROUPS):
            qv = heads(q, t, gi)
            r = 0 if gi == 0 else t
            kc = caches[gi][0, :, r, 0]
            vc = caches[gi][0, :, r, 1]
            s_c = jnp.sum(kc * qv[None], axis=-1, keepdims=True) * scale
            if gi == 0:
                s_c = jnp.where(row_c >= t, s_c, -jnp.inf)
                news = list(range(t + 1))
            else:
                news = [t]
            s_n = [jnp.sum(k_new[gi][tn] * qv, axis=-1, keepdims=True) * scale for tn in news]
            m = jnp.max(s_c, axis=0)
            for sn in s_n:
                m = jnp.maximum(m, sn)
            p_c = jnp.exp(s_c - m[None])
            l = jnp.sum(p_c, axis=0)
            o = jnp.sum(p_c * vc, axis=0)
            for tn, sn in zip(news, s_n):
                p_n = jnp.exp(sn - m)
                l = l + p_n
                o = o + p_n * v_new[gi][tn]
            parts.append((o, m, l))
        m_all = jnp.maximum(jnp.maximum(parts[0][1], parts[1][1]), parts[2][1])
        num = 0.0
        den = 0.0
        for (o, m, l) in parts:
            e = jnp.exp(m - m_all)
            num = num + e * o
            den = den + e * l
        out = (num / den).astype(o_ref.dtype)
        for h in range(H_G):
            o_ref[0, t:t + 1, h * HD_B:(h + 1) * HD_B] = out[h:h + 1, :]


def _attn_sample(proj, off, caches, windows, l):
    b, t, _ = proj.shape
    depth = caches[0].shape[0]
    assert t <= DIL_GROUPS[1][1] and DIL_GROUPS[0][1] == 1
    views, cspecs = [], []
    for c, (win, dil) in zip(caches, DIL_GROUPS):
        views.append(c.reshape(depth, b, N_BACK, dil, 2, H_G, HD_B))
        rows = min(dil, t)
        cspecs.append(pl.BlockSpec((None, 1, N_BACK, rows, 2, H_G, HD_B),
                                   lambda bi: (l, bi, 0, 0, 0, 0, 0)))
    wspecs = [pl.BlockSpec((None, 1, t, 2, H_G, HD_B),
                           functools.partial(lambda bi, blk: (l, bi, blk, 0, 0, 0), blk=win // t - 1))
              for (win, _) in DIL_GROUPS]

    def pspec(name):
        return pl.BlockSpec((1, t, QKV_B), lambda bi: (bi, 0, off[name] // QKV_B))

    any_spec = pl.BlockSpec(memory_space=pl.ANY)
    res = pl.pallas_call(
        functools.partial(_attn_sample_body, t_new=t),
        grid=(b,),
        in_specs=[pspec("qb"), pspec("kb"), pspec("vb")] + cspecs + [any_spec] * 3,
        out_specs=[pl.BlockSpec((1, t, OUT_B), lambda bi: (bi, 0, 0))] + wspecs,
        out_shape=[jax.ShapeDtypeStruct((b, t, OUT_B), BF16)]
        + [jax.ShapeDtypeStruct(w.shape, w.dtype) for w in windows],
        input_output_aliases={6: 1, 7: 2, 8: 3},
        compiler_params=_cparams(("parallel",)),
        name="attn_sample",
    )(proj, proj, proj, *views, *windows)
    return res[0], tuple(res[1:])


def _pool_body(u_ref, hist_ref, wp_ref, ps_ref, o_ref, ext_scr, *, tt, tr, pos0, n_tiles):
    hp = POOL_HIST + 1
    it = pl.program_id(1)

    @pl.when(it == 0)
    def _():
        ext_scr[0:hp, :] = hist_ref[0]

    if tr < tt:
        ext_scr[hp:, :] = jnp.zeros((tt, C_POOL), F32)
    ext_scr[hp:hp + tr, :] = u_ref[0]

    pos = pos0 + it * tt + lax.broadcasted_iota(jnp.int32, (tt, 1), 0)
    outs = []
    for gi, win in enumerate(POOL_WINDOWS):
        lo, hi = gi * CG, (gi + 1) * CG
        x = ext_scr[hp:hp + tt, lo:hi]
        acc = x
        for back in range(1, win):
            acc = acc + ext_scr[hp - back:hp - back + tt, lo:hi]
        cnt = jnp.minimum(win, pos + 1).astype(F32)
        pooled = acc / cnt - x
        outs.append(jnp.dot(pooled.astype(BF16), wp_ref[gi], preferred_element_type=F32))
    oc = jnp.concatenate(outs, axis=1) * ps_ref[...]
    o_ref[0] = oc[0:tr].astype(o_ref.dtype)

    if n_tiles > 1:
        ext_scr[0:hp, :] = ext_scr[tt:tt + hp, :]


def _pool(proj, off, hist16, w_pool, pool_scale, l, pos0, tile):
    b, t, _ = proj.shape
    hp = POOL_HIST + 1
    if t >= tile:
        tt = tr = tile
        n_tiles = t // tile
    else:
        tt, tr, n_tiles = SUBLANE, t, 1
    cu = off["uc"] // C_POOL
    assert off["uc"] % C_POOL == 0
    ng = len(POOL_WINDOWS)
    return pl.pallas_call(
        functools.partial(_pool_body, tt=tt, tr=tr, pos0=pos0, n_tiles=n_tiles),
        grid=(b, n_tiles),
        in_specs=[pl.BlockSpec((1, tr, C_POOL), lambda bi, ti: (bi, ti, cu)),
                  pl.BlockSpec((1, hp, C_POOL), lambda bi, ti: (bi, 0, 0)),
                  pl.BlockSpec((None, ng, CG, CG), lambda bi, ti: (l, 0, 0, 0)),
                  _wspec(l, 1, C_POOL, lambda bi, ti: (0, 0))],
        out_specs=pl.BlockSpec((1, tr, C_POOL), lambda bi, ti: (bi, ti, 0)),
        out_shape=jax.ShapeDtypeStruct((b, t, C_POOL), BF16),
        scratch_shapes=[pltpu.VMEM((hp + tt, C_POOL), F32)],
        compiler_params=_cparams(("parallel", "arbitrary")),
        name="pool",
    )(proj, hist16, w_pool, pool_scale)


def _tile_rows(m, pref):
    return pref if m % pref == 0 else m


def _layer(x, wts, l, conv_hist, s0, pool_hist, caches, windows, prompt):
    (w_in, conv_w, a_log, dt_bias, gdn_gain, w_pool, pool_scale, w_a, w_b, w_c, w_out, w_gu, w_down,
     g_pre_mix, g_post_mix, g_pre_ffn, g_post_ffn) = wts
    b, t, d = x.shape
    m = b * t
    off, npad = _proj_layout(d)
    x2 = x.reshape(m, d)
    tm = _tile_rows(m, 512)

    proj2 = _norm_matmul(x2, g_pre_mix, w_in, l, _tile_rows(m, 1024), 512)
    proj = proj2.reshape(b, t, npad)

    hist8 = jnp.concatenate([jnp.zeros((b, SUBLANE - (CONV_K - 1), 3 * QK_A), F32), conv_hist], axis=1)
    if prompt:
        out_a, s_new = _gdn(proj, off, conv_w, hist8, s0, a_log, dt_bias, gdn_gain, l, tile=256, chunk=256, hb=4)
    else:
        out_a, s_new = _gdn_sample(proj, off, conv_w, hist8, s0, a_log, dt_bias, gdn_gain, l)
    qa0 = off["qa"]
    conv_new = jnp.concatenate([conv_hist, proj[:, :, qa0:qa0 + 3 * QK_A]], axis=1)[:, -(CONV_K - 1):]

    kb0, vb0 = off["kb"], off["vb"]
    if prompt:
        out_b = _attn_prompt(proj, off).reshape(m, OUT_B)
        win_new = []
        for gi, (win, _) in enumerate(DIL_GROUPS):
            rows = min(win, t)
            kg = proj[:, t - rows:, kb0 + gi * OUT_B:kb0 + (gi + 1) * OUT_B].reshape(b, rows, 1, H_G, HD_B)
            vg = proj[:, t - rows:, vb0 + gi * OUT_B:vb0 + (gi + 1) * OUT_B].reshape(b, rows, 1, H_G, HD_B)
            win_new.append(jnp.concatenate([kg, vg], axis=2))
    else:
        out_b, win_new = _attn_sample(proj, off, caches, windows, l)
        out_b = out_b.reshape(m, OUT_B)

    hist16 = jnp.concatenate([jnp.zeros((b, 1, C_POOL), F32), pool_hist], axis=1)
    out_c = _pool(proj, off, hist16, w_pool, pool_scale, l, 0 if prompt else SAMPLE_POS0, 256)
    uc0 = off["uc"]
    pool_new = jnp.concatenate([pool_hist, proj[:, :, uc0:uc0 + C_POOL]], axis=1)[:, -POOL_HIST:]

    merged = _merge(out_a.reshape(m, QK_A), out_b, out_c.reshape(m, C_POOL), w_a, w_b, w_c, proj2, off, l, tm, 512)
    x2 = _mm_norm_res(merged, w_out, x2, g_post_mix, l, tm, d)
    act = _ffn_up(x2, g_pre_ffn, w_gu, l, tm, 512)
    d_ff = act.shape[1]
    x2 = _mm_norm_res(act, w_down, x2, g_post_ffn, l, tm, d_ff // 4 if d_ff % (4 * LANE) == 0 else 512)
    return x2.reshape(b, t, d), (win_new[0], win_new[1], win_new[2], s_new, conv_new, pool_new)


def kernel(x_prompt, x_sample, cache_win1, cache_win2, cache_win3, state_gdn, state_conv, state_pool,
           w_in, conv_w, a_log, dt_bias, gdn_gain, w_pool, pool_scale, w_br_a, w_br_b, w_br_c,
           w_out, w_gu, w_down, g_pre_mix, g_post_mix, g_pre_ffn, g_post_ffn):
    depth = w_in.shape[0]
    bp = x_prompt.shape[0]
    d = x_prompt.shape[-1]

    def row(v):
        return v.reshape(depth, 1, v.shape[-1])

    wts = (_prep_w_in(w_in.astype(BF16), d), conv_w, a_log, dt_bias, row(gdn_gain),
           w_pool.astype(BF16), row(pool_scale),
           w_br_a.astype(BF16), w_br_b.astype(BF16), w_br_c.astype(BF16),
           w_out.astype(BF16), w_gu.astype(BF16), w_down.astype(BF16),
           row(g_pre_mix), row(g_post_mix), row(g_pre_ffn), row(g_post_ffn))
    yp, ys = x_prompt, x_sample
    caches = (cache_win1, cache_win2, cache_win3)
    windows = tuple(_shift_cache(c, x_sample.shape[1]) for c in caches)
    new_p, new_s = [], []
    for l in range(depth):
        yp, st_p = _layer(yp, wts, l,
                          jnp.zeros((bp, CONV_K - 1, 3 * QK_A), F32),
                          jnp.zeros((bp, H_A, DK_A, DK_A), F32),
                          jnp.zeros((bp, POOL_HIST, C_POOL), F32),
                          None, None, True)
        ys, st_s = _layer(ys, wts, l, state_conv[l], state_gdn[l], state_pool[l],
                          caches, windows, False)
        windows = st_s[:3]
        new_p.append(st_p)
        new_s.append(st_s)
    outs_p = [jnp.stack([st[i] for st in new_p], axis=0) for i in range(6)]
    outs_s = [jnp.stack([st[i] for st in new_s], axis=0) for i in range(3, 6)]
    return (yp, ys, *outs_p, *windows, *outs_s)
```

```python
import functools

import jax
import jax.numpy as jnp
from jax import lax
from jax.experimental import pallas as pl
from jax.experimental.pallas import tpu as pltpu

F32 = jnp.float32
BF16 = jnp.bfloat16

H_A = 8
DK_A = 128
CONV_K = 4
DIL_GROUPS = ((128, 1), (512, 4), (2048, 16))
N_GROUPS = len(DIL_GROUPS)
H_G = 4
HD_B = 128
N_BACK = 128
POOL_WINDOWS = (2, 4, 8, 16)
CG = 256
C_POOL = CG * len(POOL_WINDOWS)
POOL_HIST = max(POOL_WINDOWS) - 1
SAMPLE_POS0 = 8192
EPS = 1e-6
L2_EPS = 1e-6

LANE = 128
SUBLANE = 8
VMEM_LIMIT = 48 * 1024 * 1024

QK_A = H_A * DK_A
QKV_B = N_GROUPS * H_G * HD_B
OUT_B = H_G * HD_B
GDN_TILE = 256
Q_TILE = 128
ATTN_SEGMENTS = 4
WRONG_RESIDUE = -(1 << 24)


def _proj_layout(d_model):
    off = {}
    c = 0
    for name, width in (("qb", QKV_B), ("kb", QKV_B), ("vb", QKV_B),
                        ("ba", H_A), ("aa", H_A), ("pad0", 512 - 2 * H_A),
                        ("uc", C_POOL),
                        ("qa", QK_A), ("ka", QK_A), ("va", QK_A), ("za", QK_A),
                        ("ga", d_model), ("gb", d_model), ("gc", d_model)):
        off[name] = c
        c += width
    total = -(-c // 512) * 512
    return off, total


W_PREP_ROWS = 128


def _prep_w_body(w_ref, o_ref, *, segs):
    pos = 0
    for src, width in segs:
        if src is None:
            o_ref[:, pos:pos + width] = jnp.zeros((o_ref.shape[0], width), BF16)
        else:
            o_ref[:, pos:pos + width] = w_ref[:, src:src + width].astype(BF16)
        pos += width


def _prep_w_in(w, d_model):
    depth, d, n_in = w.shape
    src = {}
    c = 0
    for name, width in (("qa", QK_A), ("ka", QK_A), ("va", QK_A), ("za", QK_A), ("ba", H_A), ("aa", H_A),
                        ("qb", QKV_B), ("kb", QKV_B), ("vb", QKV_B), ("uc", C_POOL),
                        ("ga", d_model), ("gb", d_model), ("gc", d_model)):
        src[name] = (c, width)
        c += width
    assert c == n_in
    off, total = _proj_layout(d_model)
    names = sorted(off, key=off.get)
    segs = []
    for i, name in enumerate(names):
        end = off[names[i + 1]] if i + 1 < len(names) else total
        if name in src:
            s0, width = src[name]
            if segs and segs[-1][0] is not None and segs[-1][0] + segs[-1][1] == s0:
                segs[-1] = (segs[-1][0], segs[-1][1] + width)
            else:
                segs.append((s0, width))
            pad = end - off[name] - width
        else:
            pad = end - off[name]
        if pad:
            segs.append((None, pad))
    tr = W_PREP_ROWS
    return pl.pallas_call(
        functools.partial(_prep_w_body, segs=tuple(segs)),
        grid=(depth, d // tr),
        in_specs=[pl.BlockSpec((None, tr, n_in), lambda l, i: (l, i, 0))],
        out_specs=pl.BlockSpec((None, tr, total), lambda l, i: (l, i, 0)),
        out_shape=jax.ShapeDtypeStruct((depth, d, total), BF16),
        compiler_params=_cparams(("parallel", "parallel")),
        name="prep_w_in",
    )(w)


def _cparams(sem):
    return pltpu.CompilerParams(dimension_semantics=sem, vmem_limit_bytes=VMEM_LIMIT)


def _dot(a, b):
    return jnp.dot(a.astype(BF16), b.astype(BF16), preferred_element_type=F32)


def _dot_nt(a, b):
    return lax.dot_general(a.astype(BF16), b.astype(BF16), (((1,), (1,)), ((), ())),
                           preferred_element_type=F32)


def _dot_tn(a, b):
    return lax.dot_general(a.astype(BF16), b.astype(BF16), (((0,), (0,)), ((), ())),
                           preferred_element_type=F32)


def _split3(x):
    x1 = x.astype(BF16).astype(F32)
    r1 = x - x1
    x2 = r1.astype(BF16).astype(F32)
    x3 = (r1 - x2).astype(BF16).astype(F32)
    return x1, x2, x3


def _rms_scale(y):
    return lax.rsqrt(jnp.mean(y * y, axis=-1, keepdims=True) + EPS)


def _silu(x):
    return x * jax.nn.sigmoid(x)


def _wspec(l, rows, cols, index):
    return pl.BlockSpec((None, rows, cols), lambda *g: (l,) + index(*g))


def _norm_mm_body(x_ref, g_ref, w_ref, o_ref, h_ref):
    @pl.when(pl.program_id(1) == 0)
    def _():
        xf = x_ref[...]
        h_ref[...] = (xf * _rms_scale(xf) * g_ref[...]).astype(BF16)

    o_ref[...] = jnp.dot(h_ref[...], w_ref[...], preferred_element_type=F32)


def _norm_matmul(x, gain, w, l, tm, tn):
    m, d = x.shape
    n = w.shape[-1]
    return pl.pallas_call(
        _norm_mm_body,
        grid=(m // tm, n // tn),
        in_specs=[pl.BlockSpec((tm, d), lambda i, j: (i, 0)),
                  _wspec(l, 1, d, lambda i, j: (0, 0)),
                  _wspec(l, d, tn, lambda i, j: (0, j))],
        out_specs=pl.BlockSpec((tm, tn), lambda i, j: (i, j)),
        out_shape=jax.ShapeDtypeStruct((m, n), F32),
        scratch_shapes=[pltpu.VMEM((tm, d), BF16)],
        compiler_params=_cparams(("parallel", "arbitrary")),
        name="in_proj",
    )(x, gain, w)


def _ffn_up_body(x_ref, g_ref, wg_ref, wu_ref, o_ref, h_ref):
    @pl.when(pl.program_id(1) == 0)
    def _():
        xf = x_ref[...]
        h_ref[...] = (xf * _rms_scale(xf) * g_ref[...]).astype(BF16)

    h = h_ref[...]
    gate = jnp.dot(h, wg_ref[...], preferred_element_type=F32)
    up = jnp.dot(h, wu_ref[...], preferred_element_type=F32)
    o_ref[...] = (_silu(gate) * up).astype(BF16)


def _ffn_up(x, gain, w_gu, l, tm, tn):
    m, d = x.shape
    d_ff = w_gu.shape[-1] // 2
    nj = d_ff // tn
    return pl.pallas_call(
        _ffn_up_body,
        grid=(m // tm, nj),
        in_specs=[pl.BlockSpec((tm, d), lambda i, j: (i, 0)),
                  _wspec(l, 1, d, lambda i, j: (0, 0)),
                  _wspec(l, d, tn, lambda i, j: (0, j)),
                  _wspec(l, d, tn, lambda i, j: (0, j + nj))],
        out_specs=pl.BlockSpec((tm, tn), lambda i, j: (i, j)),
        out_shape=jax.ShapeDtypeStruct((m, d_ff), BF16),
        scratch_shapes=[pltpu.VMEM((tm, d), BF16)],
        compiler_params=_cparams(("parallel", "arbitrary")),
        name="ffn_up",
    )(x, gain, w_gu, w_gu)


def _mm_norm_res_body(a_ref, w_ref, res_ref, g_ref, o_ref, acc_ref, *, nk):
    k = pl.program_id(1)

    @pl.when(k == 0)
    def _():
        acc_ref[...] = jnp.zeros_like(acc_ref)

    acc_ref[...] += jnp.dot(a_ref[...], w_ref[...], preferred_element_type=F32)

    @pl.when(k == nk - 1)
    def _():
        y = acc_ref[...]
        o_ref[...] = res_ref[...] + y * _rms_scale(y) * g_ref[...]


def _mm_norm_res_single_body(a_ref, w_ref, res_ref, g_ref, o_ref):
    y = jnp.dot(a_ref[...], w_ref[...], preferred_element_type=F32)
    o_ref[...] = res_ref[...] + y * _rms_scale(y) * g_ref[...]


def _mm_norm_res(a, w, res, gain, l, tm, tk):
    m, kdim = a.shape
    d = w.shape[-1]
    nk = kdim // tk
    assert nk * tk == kdim
    if nk == 1:
        return pl.pallas_call(
            _mm_norm_res_single_body,
            grid=(m // tm,),
            in_specs=[pl.BlockSpec((tm, kdim), lambda i: (i, 0)),
                      _wspec(l, kdim, d, lambda i: (0, 0)),
                      pl.BlockSpec((tm, d), lambda i: (i, 0)),
                      _wspec(l, 1, d, lambda i: (0, 0))],
            out_specs=pl.BlockSpec((tm, d), lambda i: (i, 0)),
            out_shape=jax.ShapeDtypeStruct((m, d), F32),
            compiler_params=_cparams(("parallel",)),
            name="mm_norm_res",
        )(a, w, res, gain)
    return pl.pallas_call(
        functools.partial(_mm_norm_res_body, nk=nk),
        grid=(m // tm, nk),
        in_specs=[pl.BlockSpec((tm, tk), lambda i, k: (i, k)),
                  _wspec(l, tk, d, lambda i, k: (k, 0)),
                  pl.BlockSpec((tm, d), lambda i, k: (i, 0)),
                  _wspec(l, 1, d, lambda i, k: (0, 0))],
        out_specs=pl.BlockSpec((tm, d), lambda i, k: (i, 0)),
        out_shape=jax.ShapeDtypeStruct((m, d), F32),
        scratch_shapes=[pltpu.VMEM((tm, d), F32)],
        compiler_params=_cparams(("parallel", "arbitrary")),
        name="mm_norm_res",
    )(a, w, res, gain)


def _merge_body(a_ref, b_ref, c_ref, wa_ref, wb_ref, wc_ref, ga_ref, gb_ref, gc_ref, o_ref):
    m = jax.nn.sigmoid(ga_ref[...]) * jnp.dot(a_ref[...], wa_ref[...], preferred_element_type=F32)
    m = m + jax.nn.sigmoid(gb_ref[...]) * jnp.dot(b_ref[...], wb_ref[...], preferred_element_type=F32)
    m = m + jax.nn.sigmoid(gc_ref[...]) * jnp.dot(c_ref[...], wc_ref[...], preferred_element_type=F32)
    o_ref[...] = m.astype(BF16)


def _merge(out_a, out_b, out_c, w_a, w_b, w_c, proj2d, off, l, tm, tn):
    m = out_a.shape[0]
    d = w_a.shape[-1]
    ja, jb, jc = off["ga"] // tn, off["gb"] // tn, off["gc"] // tn
    return pl.pallas_call(
        _merge_body,
        grid=(m // tm, d // tn),
        in_specs=[pl.BlockSpec((tm, out_a.shape[1]), lambda i, j: (i, 0)),
                  pl.BlockSpec((tm, out_b.shape[1]), lambda i, j: (i, 0)),
                  pl.BlockSpec((tm, out_c.shape[1]), lambda i, j: (i, 0)),
                  _wspec(l, w_a.shape[1], tn, lambda i, j: (0, j)),
                  _wspec(l, w_b.shape[1], tn, lambda i, j: (0, j)),
                  _wspec(l, w_c.shape[1], tn, lambda i, j: (0, j)),
                  pl.BlockSpec((tm, tn), lambda i, j: (i, j + ja)),
                  pl.BlockSpec((tm, tn), lambda i, j: (i, j + jb)),
                  pl.BlockSpec((tm, tn), lambda i, j: (i, j + jc))],
        out_specs=pl.BlockSpec((tm, tn), lambda i, j: (i, j)),
        out_shape=jax.ShapeDtypeStruct((m, d), BF16),
        compiler_params=_cparams(("parallel", "arbitrary")),
        name="merge",
    )(out_a, out_b, out_c, w_a, w_b, w_c, proj2d, proj2d, proj2d)


def _unit_lower_inverse(lower, c):
    n = lower.shape[0]
    row = lax.broadcasted_iota(jnp.int32, (n, n), 0)
    col = lax.broadcasted_iota(jnp.int32, (n, n), 1)

    def same_block(shift):
        return (row >> shift) == (col >> shift)

    eye = jnp.where(row == col, 1.0, 0.0).astype(F32)
    d1 = jnp.where(same_block(3), lower, 0.0)
    d2 = _dot(d1, d1)
    d4 = _dot(d2, d2)
    x = eye - d1
    x = x + _dot(d2, x)
    x = x + _dot(d4, x)
    shift = 3
    while (1 << shift) < c:
        off_diag = jnp.where(jnp.logical_and(same_block(shift + 1), jnp.logical_not(same_block(shift))),
                             lower, 0.0)
        x = x - _dot(_dot(x, off_diag), x)
        shift += 1
    return x


def _unit_lower_inverses(lowers, c):
    n = lowers[0].shape[0]
    row = lax.broadcasted_iota(jnp.int32, (n, n), 0)
    col = lax.broadcasted_iota(jnp.int32, (n, n), 1)

    def same_block(shift):
        return (row >> shift) == (col >> shift)

    eye = jnp.where(row == col, 1.0, 0.0).astype(F32)
    blk = same_block(3)
    d1 = [jnp.where(blk, lo, 0.0) for lo in lowers]
    d2 = [_dot(d, d) for d in d1]
    d4 = [_dot(d, d) for d in d2]
    xs = [eye - d for d in d1]
    xs = [x + _dot(d, x) for d, x in zip(d2, xs)]
    xs = [x + _dot(d, x) for d, x in zip(d4, xs)]
    shift = 3
    while (1 << shift) < c:
        pick = jnp.logical_and(same_block(shift + 1), jnp.logical_not(same_block(shift)))
        offs = [jnp.where(pick, lo, 0.0) for lo in lowers]
        ts = [_dot(x, o) for x, o in zip(xs, offs)]
        xs = [x - _dot(t, x) for t, x in zip(ts, xs)]
        shift += 1
    return xs


def _gdn_prompt_body(alog_ref, dtb_ref, q_ref, k_ref, v_ref, z_ref, ba_ref, cwq_ref, cwk_ref, cwv_ref,
                     hq_ref, hk_ref, hv_ref, s0_ref, gain_ref, o_ref, sout_ref,
                     s_scr, ext_scr, *, tt, n_tiles, hb):
    c = tt
    h0 = pl.program_id(1) * hb
    it = pl.program_id(2)
    heads = range(hb)

    @pl.when(it == 0)
    def _():
        s_scr[...] = s0_ref[0]
        ext_scr[0, 0:SUBLANE, :] = hq_ref[0]
        ext_scr[1, 0:SUBLANE, :] = hk_ref[0]
        ext_scr[2, 0:SUBLANE, :] = hv_ref[0]

    ext_scr[0, SUBLANE:, :] = q_ref[0]
    ext_scr[1, SUBLANE:, :] = k_ref[0]
    ext_scr[2, SUBLANE:, :] = v_ref[0]

    def conv_act(i, cw_ref):
        base = SUBLANE - (CONV_K - 1)
        y = ext_scr[i, base:base + tt, :] * cw_ref[0:1, :]
        for j in range(1, CONV_K):
            y = y + ext_scr[i, base + j:base + j + tt, :] * cw_ref[j:j + 1, :]
        return _silu(y)

    q_all = conv_act(0, cwq_ref)
    k_all = conv_act(1, cwk_ref)
    v_all = conv_act(2, cwv_ref)

    if n_tiles > 1:
        for i in range(3):
            ext_scr[i, 0:SUBLANE, :] = ext_scr[i, tt:tt + SUBLANE, :]

    ba = ba_ref[0]
    lane = lax.broadcasted_iota(jnp.int32, (tt, LANE), 1)
    lane_c = lax.broadcasted_iota(jnp.int32, (c, LANE), 1)
    row = lax.broadcasted_iota(jnp.int32, (c, c), 0)
    col = lax.broadcasted_iota(jnp.int32, (c, c), 1)
    incl = row >= col
    strict = row > col
    eye = row == col
    tri = jnp.where(incl, 1.0, 0.0).astype(BF16)

    def hsl(hh):
        return slice(hh * LANE, (hh + 1) * LANE)

    qs, ks, vs, betas, g_terms = [], [], [], [], []
    for hh in heads:
        q = q_all[:, hsl(hh)]
        k = k_all[:, hsl(hh)]
        qs.append(q * lax.rsqrt(jnp.sum(q * q, axis=-1, keepdims=True) + L2_EPS) * (DK_A ** -0.5))
        ks.append(k * lax.rsqrt(jnp.sum(k * k, axis=-1, keepdims=True) + L2_EPS))
        vs.append(v_all[:, hsl(hh)])
        b_raw = jnp.sum(jnp.where(lane == h0 + hh, ba, 0.0), axis=1, keepdims=True)
        a_raw = jnp.sum(jnp.where(lane == h0 + hh + H_A, ba, 0.0), axis=1, keepdims=True)
        betas.append(jax.nn.sigmoid(b_raw))
        a_log = jnp.full((1, 1), alog_ref[h0 + hh], F32)
        xs = a_raw + dtb_ref[h0 + hh]
        softplus = jnp.maximum(xs, 0.0) + jnp.log1p(jnp.exp(-jnp.abs(xs)))
        g = -jnp.exp(a_log) * softplus
        g1, g2, g3 = _split3(jnp.broadcast_to(g, (c, LANE)))
        g_terms.append(jnp.where(lane_c == 0, g1, jnp.where(lane_c == 1, g2, g3)).astype(BF16))

    parts = [jnp.dot(tri, gt, preferred_element_type=F32) for gt in g_terms]
    kbs = [k * b for k, b in zip(ks, betas)]
    kqs = [_dot_nt(jnp.concatenate([kb, q], axis=0), k) for kb, q, k in zip(kbs, qs, ks)]
    gc_cols, decays = [], []
    for part in parts:
        gc_col = part[:, 0:1] + part[:, 1:2] + part[:, 2:3]
        gc_cb = jnp.broadcast_to(gc_col, (c, c))
        gc_rb = jnp.sum(jnp.where(eye, gc_cb, 0.0), axis=0, keepdims=True)
        decays.append(jnp.where(incl, jnp.exp(jnp.where(incl, gc_cb - gc_rb, 0.0)), 0.0))
        gc_cols.append(gc_col)
    lowers = [jnp.where(strict, kq[0:c] * dec, 0.0) for kq, dec in zip(kqs, decays)]
    a_intras = [jnp.where(incl, kq[c:2 * c] * dec, 0.0) for kq, dec in zip(kqs, decays)]
    tinvs = _unit_lower_inverses(lowers, c)
    egcs = [jnp.exp(gc) for gc in gc_cols]
    uws = [_dot(tinv, jnp.concatenate([v * b, kb * egc], axis=1))
           for tinv, v, b, kb, egc in zip(tinvs, vs, betas, kbs, egcs)]
    s_old = [s_scr[hh] for hh in heads]
    ws_qs = [_dot(jnp.concatenate([uw[:, DK_A:2 * DK_A], q * egc], axis=0), s)
             for uw, q, egc, s in zip(uws, qs, egcs, s_old)]
    v_news = [uw[:, 0:DK_A] - wq[0:c] for uw, wq in zip(uws, ws_qs)]
    outs = [wq[c:2 * c] + _dot(a, vn) for wq, a, vn in zip(ws_qs, a_intras, v_news)]
    for hh in heads:
        g_last = gc_cols[hh][c - 1:c, :]
        kd = ks[hh] * jnp.exp(g_last - gc_cols[hh])
        s_scr[hh] = s_old[hh] * jnp.exp(g_last) + _dot_tn(kd, v_news[hh])
    z_all = z_ref[0]
    for hh in heads:
        o = outs[hh]
        o = o * _rms_scale(o) * gain_ref[...]
        o = o * _silu(z_all[:, hsl(hh)])
        o_ref[0, :, hsl(hh)] = o.astype(o_ref.dtype)

    @pl.when(it == n_tiles - 1)
    def _():
        sout_ref[0] = s_scr[...]


def _gdn(proj, off, conv_w, conv_hist8, s0, a_log, dt_bias, gdn_gain, l, *, tile, hb):
    b, t, _ = proj.shape
    assert t % tile == 0
    tt = tr = tile
    n_tiles = t // tile
    wid = hb * LANE
    cq, ck, cv, cz = (off[n] // wid for n in ("qa", "ka", "va", "za"))
    cba = off["ba"] // LANE
    nhb = H_A // hb

    def tok(col0):
        return pl.BlockSpec((1, tr, wid), lambda bi, hi, ti: (bi, ti, col0 + hi))

    def per_head(col0, rows):
        return _wspec(l, rows, wid, lambda bi, hi, ti: (0, col0 + hi))

    def hist(col0):
        return pl.BlockSpec((1, SUBLANE, wid), lambda bi, hi, ti: (bi, 0, col0 + hi))

    smem = pl.BlockSpec(memory_space=pltpu.SMEM)
    state_spec = pl.BlockSpec((1, hb, DK_A, DK_A), lambda bi, hi, ti: (bi, hi, 0, 0))
    body = functools.partial(_gdn_prompt_body, tt=tt, n_tiles=n_tiles, hb=hb)
    return pl.pallas_call(
        body,
        grid=(b, nhb, n_tiles),
        in_specs=[smem, smem,
                  tok(cq), tok(ck), tok(cv), tok(cz),
                  pl.BlockSpec((1, tr, LANE), lambda bi, hi, ti: (bi, ti, cba)),
                  per_head(0, CONV_K), per_head(nhb, CONV_K), per_head(2 * nhb, CONV_K),
                  hist(0), hist(nhb), hist(2 * nhb),
                  state_spec,
                  _wspec(l, 1, DK_A, lambda bi, hi, ti: (0, 0))],
        out_specs=[pl.BlockSpec((1, tr, wid), lambda bi, hi, ti: (bi, ti, hi)),
                   state_spec],
        out_shape=[jax.ShapeDtypeStruct((b, t, QK_A), BF16),
                   jax.ShapeDtypeStruct((b, H_A, DK_A, DK_A), F32)],
        scratch_shapes=[pltpu.VMEM((hb, DK_A, DK_A), F32),
                        pltpu.VMEM((3, tt + SUBLANE, wid), F32)],
        compiler_params=_cparams(("parallel", "parallel", "arbitrary")),
        name="gdn",
    )(a_log[l], dt_bias[l], proj, proj, proj, proj, proj, conv_w, conv_w, conv_w,
      conv_hist8, conv_hist8, conv_hist8, s0, gdn_gain)


def _gdn_sample_body(alog_ref, dtb_ref, q_ref, k_ref, v_ref, z_ref, ba_ref, cwq_ref, cwk_ref, cwv_ref,
                     hq_ref, hk_ref, hv_ref, s0_ref, gain_ref, o_ref, sout_ref, ext_scr, z_scr, ba_scr, *, tr):
    tp = SUBLANE
    n = H_A * tp
    for i, (x_ref, h_ref) in enumerate(((q_ref, hq_ref), (k_ref, hk_ref), (v_ref, hv_ref))):
        ext_scr[i, 0:SUBLANE, :] = h_ref[0]
        ext_scr[i, SUBLANE:, :] = jnp.zeros((tp, QK_A), F32)
        ext_scr[i, SUBLANE:SUBLANE + tr, :] = x_ref[0]

    def conv_act(i, cw_ref):
        base = SUBLANE - (CONV_K - 1)
        y = ext_scr[i, base:base + tp, :] * cw_ref[0:1, :]
        for j in range(1, CONV_K):
            y = y + ext_scr[i, base + j:base + j + tp, :] * cw_ref[j:j + 1, :]
        return _silu(y)

    def pack(x):
        return jnp.concatenate([x[:, h * LANE:(h + 1) * LANE] for h in range(H_A)], axis=0)

    q = pack(conv_act(0, cwq_ref))
    k = pack(conv_act(1, cwk_ref))
    v = pack(conv_act(2, cwv_ref))
    q = q * lax.rsqrt(jnp.sum(q * q, axis=-1, keepdims=True) + L2_EPS) * (DK_A ** -0.5)
    k = k * lax.rsqrt(jnp.sum(k * k, axis=-1, keepdims=True) + L2_EPS)

    ba_scr[...] = jnp.zeros((tp, LANE), F32)
    ba_scr[0:tr, :] = ba_ref[0]
    ba = ba_scr[...]
    b_raw = jnp.concatenate([ba[:, h:h + 1] for h in range(H_A)], axis=0)
    a_raw = jnp.concatenate([ba[:, H_A + h:H_A + h + 1] for h in range(H_A)], axis=0)
    a_log = jnp.concatenate([jnp.full((tp, 1), alog_ref[h], F32) for h in range(H_A)], axis=0)
    dtb = jnp.concatenate([jnp.full((tp, 1), dtb_ref[h], F32) for h in range(H_A)], axis=0)
    beta = jax.nn.sigmoid(b_raw)
    xs = a_raw + dtb
    g = -jnp.exp(a_log) * (jnp.maximum(xs, 0.0) + jnp.log1p(jnp.exp(-jnp.abs(xs))))

    valid = (lax.broadcasted_iota(jnp.int32, (n, 1), 0) & (tp - 1)) < tr
    q = jnp.where(valid, q, 0.0)
    k = jnp.where(valid, k, 0.0)
    v = jnp.where(valid, v, 0.0)
    beta = jnp.where(valid, beta, 0.0)
    g = jnp.where(valid, g, 0.0)

    row = lax.broadcasted_iota(jnp.int32, (n, n), 0)
    col = lax.broadcasted_iota(jnp.int32, (n, n), 1)
    same_head = (row >> 3) == (col >> 3)
    incl = jnp.logical_and(same_head, row >= col)
    strict = jnp.logical_and(same_head, row > col)
    eye = row == col
    sum_mat = jnp.concatenate([jnp.where(incl, 1.0, 0.0), jnp.where(same_head, 1.0, 0.0)], axis=0).astype(BF16)
    g1, g2, g3 = _split3(jnp.broadcast_to(g, (n, LANE)))
    lane_c = lax.broadcasted_iota(jnp.int32, (n, LANE), 1)
    g_terms = jnp.where(lane_c == 0, g1, jnp.where(lane_c == 1, g2, g3))
    part = jnp.dot(sum_mat, g_terms.astype(BF16), preferred_element_type=F32)
    sums = part[:, 0:1] + part[:, 1:2] + part[:, 2:3]
    gc_col = sums[0:n]
    g_last = sums[n:2 * n]
    gc_cb = jnp.broadcast_to(gc_col, (n, n))
    gc_rb = jnp.sum(jnp.where(eye, gc_cb, 0.0), axis=0, keepdims=True)
    decay = jnp.where(incl, jnp.exp(jnp.where(incl, gc_cb - gc_rb, 0.0)), 0.0)
    kb = k * beta
    kq = _dot_nt(jnp.concatenate([kb, q], axis=0), k)
    lower = jnp.where(strict, kq[0:n] * decay, 0.0)
    a_intra = jnp.where(incl, kq[n:2 * n] * decay, 0.0)
    tinv = _unit_lower_inverse(lower, tp)
    egc = jnp.exp(gc_col)
    uw = _dot(tinv, jnp.concatenate([v * beta, kb * egc], axis=1))
    kd = k * jnp.exp(g_last - gc_col)

    row_e = lax.broadcasted_iota(jnp.int32, (n, QK_A), 0)
    col_e = lax.broadcasted_iota(jnp.int32, (n, QK_A), 1)
    own = (row_e >> 3) == (col_e >> 7)

    def expand(x):
        return jnp.where(own, jnp.concatenate([x] * H_A, axis=1), 0.0)

    s = s0_ref[0].reshape(H_A * DK_A, DK_A)
    ws_qs = _dot(jnp.concatenate([expand(uw[:, DK_A:2 * DK_A]), expand(q * egc)], axis=0), s)
    v_new = uw[:, 0:DK_A] - ws_qs[0:n]
    o = ws_qs[n:2 * n] + _dot(a_intra, v_new)
    upd = _dot_tn(expand(kd), v_new)
    for h in range(H_A):
        eg = jnp.exp(g_last[h * tp:h * tp + 1, :])
        sout_ref[0, h] = s[h * DK_A:(h + 1) * DK_A] * eg + upd[h * DK_A:(h + 1) * DK_A]

    o = o * _rms_scale(o) * gain_ref[...]
    z_scr[...] = jnp.zeros((tp, QK_A), F32)
    z_scr[0:tr, :] = z_ref[0]
    o = o * _silu(pack(z_scr[...]))
    o_tok = jnp.concatenate([o[h * tp:(h + 1) * tp] for h in range(H_A)], axis=1)
    o_ref[0] = o_tok[0:tr].astype(o_ref.dtype)


def _gdn_sample(proj, off, conv_w, conv_hist8, s0, a_log, dt_bias, gdn_gain, l):
    b, t, _ = proj.shape
    assert t <= SUBLANE
    cq, ck, cv, cz = (off[n] // QK_A for n in ("qa", "ka", "va", "za"))
    cba = off["ba"] // LANE

    def tok(col0):
        return pl.BlockSpec((1, t, QK_A), lambda bi: (bi, 0, col0))

    def conv(col0):
        return _wspec(l, CONV_K, QK_A, lambda bi: (0, col0))

    def hist(col0):
        return pl.BlockSpec((1, SUBLANE, QK_A), lambda bi: (bi, 0, col0))

    smem = pl.BlockSpec(memory_space=pltpu.SMEM)
    state_spec = pl.BlockSpec((1, H_A, DK_A, DK_A), lambda bi: (bi, 0, 0, 0))
    return pl.pallas_call(
        functools.partial(_gdn_sample_body, tr=t),
        grid=(b,),
        in_specs=[smem, smem, tok(cq), tok(ck), tok(cv), tok(cz),
                  pl.BlockSpec((1, t, LANE), lambda bi: (bi, 0, cba)),
                  conv(0), conv(1), conv(2), hist(0), hist(1), hist(2),
                  state_spec, _wspec(l, 1, DK_A, lambda bi: (0, 0))],
        out_specs=[pl.BlockSpec((1, t, QK_A), lambda bi: (bi, 0, 0)), state_spec],
        out_shape=[jax.ShapeDtypeStruct((b, t, QK_A), BF16),
                   jax.ShapeDtypeStruct((b, H_A, DK_A, DK_A), F32)],
        scratch_shapes=[pltpu.VMEM((3, 2 * SUBLANE, QK_A), F32),
                        pltpu.VMEM((SUBLANE, QK_A), F32),
                        pltpu.VMEM((SUBLANE, LANE), F32)],
        compiler_params=_cparams(("parallel",)),
        name="gdn_sample",
    )(a_log[l], dt_bias[l], proj, proj, proj, proj, proj, conv_w, conv_w, conv_w,
      conv_hist8, conv_hist8, conv_hist8, s0, gdn_gain)


def _attn_body(q0_ref, q1_ref, q2_ref, k0_ref, k1_ref, k2_ref, v0_ref, v1_ref, v2_ref, o_ref,
               kb_scr, vb_scr, d0_scr, d1_scr, d2_scr, *, s_len):
    q_refs = (q0_ref, q1_ref, q2_ref)
    k_refs = (k0_ref, k1_ref, k2_ref)
    v_refs = (v0_ref, v1_ref, v2_ref)
    dist_scrs = (d0_scr, d1_scr, d2_scr)
    tq_n = Q_TILE
    scale = HD_B ** -0.5
    for gi, (win, dil) in enumerate(DIL_GROUPS):
        kb_scr[gi] = k_refs[gi][0].astype(BF16)
        vb_scr[gi] = v_refs[gi][0].astype(BF16)
        nk = dist_scrs[gi].shape[1]
        rc = lax.broadcasted_iota(jnp.int32, (tq_n, nk), 0) - lax.broadcasted_iota(jnp.int32, (tq_n, nk), 1)
        dist_scrs[gi][...] = jnp.where((rc & (dil - 1)) == 0, rc, WRONG_RESIDUE).astype(F32)

    def tile(i, carry, cap):
        q0 = pl.multiple_of(i * tq_n, tq_n)
        scores = []
        starts = []
        m = None
        for gi, (win, dil) in enumerate(DIL_GROUPS):
            nk = min(cap, win + tq_n)
            k0 = pl.multiple_of(jnp.clip(q0 - win, 0, cap - nk), tq_n)
            q = q_refs[gi][0, pl.ds(q0, tq_n), :]
            s = _dot_nt(q, kb_scr[gi, pl.ds(k0, nk), :]) * scale
            d = dist_scrs[gi][:, 0:nk] + (q0 - k0).astype(F32)
            ok = d >= 0.0
            if win < s_len - 1:
                ok = jnp.logical_and(ok, d <= float(win))
            s = jnp.where(ok, s, -jnp.inf)
            mg = jnp.max(s, axis=1, keepdims=True)
            m = mg if m is None else jnp.maximum(m, mg)
            scores.append(s)
            starts.append((k0, nk))
        acc = jnp.zeros((tq_n, HD_B), F32)
        den = jnp.zeros((tq_n, 1), F32)
        for gi in range(N_GROUPS):
            k0, nk = starts[gi]
            p = jnp.exp(scores[gi] - m)
            den = den + jnp.sum(p, axis=1, keepdims=True)
            acc = acc + _dot(p, vb_scr[gi, pl.ds(k0, nk), :])
        o_ref[0, pl.ds(q0, tq_n), :] = (acc / den).astype(o_ref.dtype)
        return carry

    n_tiles = s_len // tq_n
    seg = max(1, n_tiles // ATTN_SEGMENTS)
    for lo in range(0, n_tiles, seg):
        hi = min(lo + seg, n_tiles)
        lax.fori_loop(lo, hi, functools.partial(tile, cap=hi * tq_n), 0)


def _attn_prompt(proj, off):
    b, s_len, _ = proj.shape
    cq, ck, cv = (off[n] // LANE for n in ("qb", "kb", "vb"))

    def spec(col0, gi):
        return pl.BlockSpec((1, s_len, LANE), lambda bi, hi: (bi, 0, col0 + gi * H_G + hi))

    return pl.pallas_call(
        functools.partial(_attn_body, s_len=s_len),
        grid=(b, H_G),
        in_specs=[spec(c0, gi) for c0 in (cq, ck, cv) for gi in range(N_GROUPS)],
        out_specs=pl.BlockSpec((1, s_len, LANE), lambda bi, hi: (bi, 0, hi)),
        out_shape=jax.ShapeDtypeStruct((b, s_len, OUT_B), BF16),
        scratch_shapes=[pltpu.VMEM((N_GROUPS, s_len, HD_B), BF16),
                        pltpu.VMEM((N_GROUPS, s_len, HD_B), BF16)]
        + [pltpu.VMEM((Q_TILE, min(s_len, win + Q_TILE)), F32) for (win, _) in DIL_GROUPS],
        compiler_params=_cparams(("parallel", "parallel")),
        name="attn_prompt",
    )(*([proj] * 9))


SHIFT_UNITS = 128


def _shift_body(a_ref, b_ref, o_ref):
    r = a_ref.shape[0]
    o_ref[0:r - 1] = a_ref[1:r]
    o_ref[r - 1:r] = b_ref[...]


def _shift_cache(c, t_new):
    depth, batch, win = c.shape[:3]
    nu = win // t_new
    r = min(SHIFT_UNITS, nu)
    assert win % t_new == 0 and nu % r == 0
    view = c.reshape(depth, batch, nu, t_new, 2, H_G, HD_B)
    tail = (t_new, 2, H_G, HD_B)
    zeros = (0,) * len(tail)
    out = pl.pallas_call(
        _shift_body,
        grid=(depth, batch, nu // r),
        in_specs=[pl.BlockSpec((None, None, r) + tail, lambda l, b, j: (l, b, j) + zeros),
                  pl.BlockSpec((None, None, 1) + tail,
                               lambda l, b, j: (l, b, jnp.minimum((j + 1) * r, nu - 1)) + zeros)],
        out_specs=pl.BlockSpec((None, None, r) + tail, lambda l, b, j: (l, b, j) + zeros),
        out_shape=jax.ShapeDtypeStruct(view.shape, view.dtype),
        compiler_params=_cparams(("parallel", "parallel", "parallel")),
        name="cache_shift",
    )(view, view)
    return out.reshape(c.shape)


def _attn_sample_body(q_ref, kn_ref, vn_ref, c1_ref, c2_ref, c3_ref, w1_in, w2_in, w3_in,
                      o_ref, w1_ref, w2_ref, w3_ref, *, t_new):
    del w1_in, w2_in, w3_in
    scale = HD_B ** -0.5
    q = q_ref[0]
    kn = kn_ref[0]
    vn = vn_ref[0]
    caches = (c1_ref, c2_ref, c3_ref)
    wins = (w1_ref, w2_ref, w3_ref)
    row_c = lax.broadcasted_iota(jnp.int32, (N_BACK, H_G, 1), 0)

    def heads(x, t, gi):
        c0 = gi * OUT_B
        return jnp.concatenate([x[t:t + 1, c0 + h * HD_B:c0 + (h + 1) * HD_B] for h in range(H_G)], axis=0)

    k_new = [[heads(kn, t, gi) for t in range(t_new)] for gi in range(N_GROUPS)]
    v_new = [[heads(vn, t, gi) for t in range(t_new)] for gi in range(N_GROUPS)]
    for gi in range(N_GROUPS):
        for t in range(t_new):
            wins[gi][0, t, 0] = k_new[gi][t]
            wins[gi][0, t, 1] = v_new[gi][t]

    for t in range(t_new):
        parts = []
        for gi in range(N_GROUPS):
            qv = heads(q, t, gi)
            r = 0 if gi == 0 else t
            kc = caches[gi][0, :, r, 0]
            vc = caches[gi][0, :, r, 1]
            s_c = jnp.sum(kc * qv[None], axis=-1, keepdims=True) * scale
            if gi == 0:
                s_c = jnp.where(row_c >= t, s_c, -jnp.inf)
                news = list(range(t + 1))
            else:
                news = [t]
            s_n = [jnp.sum(k_new[gi][tn] * qv, axis=-1, keepdims=True) * scale for tn in news]
            m = jnp.max(s_c, axis=0)
            for sn in s_n:
                m = jnp.maximum(m, sn)
            p_c = jnp.exp(s_c - m[None])
            l = jnp.sum(p_c, axis=0)
            o = jnp.sum(p_c * vc, axis=0)
            for tn, sn in zip(news, s_n):
                p_n = jnp.exp(sn - m)
                l = l + p_n
                o = o + p_n * v_new[gi][tn]
            parts.append((o, m, l))
        m_all = jnp.maximum(jnp.maximum(parts[0][1], parts[1][1]), parts[2][1])
        num = 0.0
        den = 0.0
        for (o, m, l) in parts:
            e = jnp.exp(m - m_all)
            num = num + e * o
            den = den + e * l
        out = (num / den).astype(o_ref.dtype)
        for h in range(H_G):
            o_ref[0, t:t + 1, h * HD_B:(h + 1) * HD_B] = out[h:h + 1, :]


def _attn_sample(proj, off, caches, windows, l):
    b, t, _ = proj.shape
    depth = caches[0].shape[0]
    assert t <= DIL_GROUPS[1][1] and DIL_GROUPS[0][1] == 1
    views, cspecs = [], []
    for c, (win, dil) in zip(caches, DIL_GROUPS):
        views.append(c.reshape(depth, b, N_BACK, dil, 2, H_G, HD_B))
        rows = min(dil, t)
        cspecs.append(pl.BlockSpec((None, 1, N_BACK, rows, 2, H_G, HD_B),
                                   lambda bi: (l, bi, 0, 0, 0, 0, 0)))
    wspecs = [pl.BlockSpec((None, 1, t, 2, H_G, HD_B),
                           functools.partial(lambda bi, blk: (l, bi, blk, 0, 0, 0), blk=win // t - 1))
              for (win, _) in DIL_GROUPS]

    def pspec(name):
        return pl.BlockSpec((1, t, QKV_B), lambda bi: (bi, 0, off[name] // QKV_B))

    any_spec = pl.BlockSpec(memory_space=pl.ANY)
    res = pl.pallas_call(
        functools.partial(_attn_sample_body, t_new=t),
        grid=(b,),
        in_specs=[pspec("qb"), pspec("kb"), pspec("vb")] + cspecs + [any_spec] * 3,
        out_specs=[pl.BlockSpec((1, t, OUT_B), lambda bi: (bi, 0, 0))] + wspecs,
        out_shape=[jax.ShapeDtypeStruct((b, t, OUT_B), BF16)]
        + [jax.ShapeDtypeStruct(w.shape, w.dtype) for w in windows],
        input_output_aliases={6: 1, 7: 2, 8: 3},
        compiler_params=_cparams(("parallel",)),
        name="attn_sample",
    )(proj, proj, proj, *views, *windows)
    return res[0], tuple(res[1:])


def _pool_body(u_ref, hist_ref, wp_ref, ps_ref, o_ref, ext_scr, *, tt, tr, pos0, n_tiles):
    hp = POOL_HIST + 1
    it = pl.program_id(1)

    @pl.when(it == 0)
    def _():
        ext_scr[0:hp, :] = hist_ref[0]

    if tr < tt:
        ext_scr[hp:, :] = jnp.zeros((tt, C_POOL), F32)
    ext_scr[hp:hp + tr, :] = u_ref[0]

    pos = pos0 + it * tt + lax.broadcasted_iota(jnp.int32, (tt, 1), 0)
    outs = []
    for gi, win in enumerate(POOL_WINDOWS):
        lo, hi = gi * CG, (gi + 1) * CG
        x = ext_scr[hp:hp + tt, lo:hi]
        acc = x
        for back in range(1, win):
            acc = acc + ext_scr[hp - back:hp - back + tt, lo:hi]
        cnt = jnp.minimum(win, pos + 1).astype(F32)
        pooled = acc / cnt - x
        outs.append(jnp.dot(pooled.astype(BF16), wp_ref[gi], preferred_element_type=F32))
    oc = jnp.concatenate(outs, axis=1) * ps_ref[...]
    o_ref[0] = oc[0:tr].astype(o_ref.dtype)

    if n_tiles > 1:
        ext_scr[0:hp, :] = ext_scr[tt:tt + hp, :]


def _pool(proj, off, hist16, w_pool, pool_scale, l, pos0, tile):
    b, t, _ = proj.shape
    hp = POOL_HIST + 1
    if t >= tile:
        tt = tr = tile
        n_tiles = t // tile
    else:
        tt, tr, n_tiles = SUBLANE, t, 1
    cu = off["uc"] // C_POOL
    assert off["uc"] % C_POOL == 0
    ng = len(POOL_WINDOWS)
    return pl.pallas_call(
        functools.partial(_pool_body, tt=tt, tr=tr, pos0=pos0, n_tiles=n_tiles),
        grid=(b, n_tiles),
        in_specs=[pl.BlockSpec((1, tr, C_POOL), lambda bi, ti: (bi, ti, cu)),
                  pl.BlockSpec((1, hp, C_POOL), lambda bi, ti: (bi, 0, 0)),
                  pl.BlockSpec((None, ng, CG, CG), lambda bi, ti: (l, 0, 0, 0)),
                  _wspec(l, 1, C_POOL, lambda bi, ti: (0, 0))],
        out_specs=pl.BlockSpec((1, tr, C_POOL), lambda bi, ti: (bi, ti, 0)),
        out_shape=jax.ShapeDtypeStruct((b, t, C_POOL), BF16),
        scratch_shapes=[pltpu.VMEM((hp + tt, C_POOL), F32)],
        compiler_params=_cparams(("parallel", "arbitrary")),
        name="pool",
    )(proj, hist16, w_pool, pool_scale)


def _tile_rows(m, pref):
    return pref if m % pref == 0 else m


def _layer(x, wts, l, conv_hist, s0, pool_hist, caches, windows, prompt):
    (w_in, conv_w, a_log, dt_bias, gdn_gain, w_pool, pool_scale, w_a, w_b, w_c, w_out, w_gu, w_down,
     g_pre_mix, g_post_mix, g_pre_ffn, g_post_ffn) = wts
    b, t, d = x.shape
    m = b * t
    off, npad = _proj_layout(d)
    x2 = x.reshape(m, d)
    tm = _tile_rows(m, 512)

    proj2 = _norm_matmul(x2, g_pre_mix, w_in, l, _tile_rows(m, 1024), 512)
    proj = proj2.reshape(b, t, npad)

    hist8 = jnp.concatenate([jnp.zeros((b, SUBLANE - (CONV_K - 1), 3 * QK_A), F32), conv_hist], axis=1)
    if prompt:
        out_a, s_new = _gdn(proj, off, conv_w, hist8, s0, a_log, dt_bias, gdn_gain, l, tile=GDN_TILE, hb=4)
    else:
        out_a, s_new = _gdn_sample(proj, off, conv_w, hist8, s0, a_log, dt_bias, gdn_gain, l)
    qa0 = off["qa"]
    conv_new = jnp.concatenate([conv_hist, proj[:, :, qa0:qa0 + 3 * QK_A]], axis=1)[:, -(CONV_K - 1):]

    kb0, vb0 = off["kb"], off["vb"]
    if prompt:
        out_b = _attn_prompt(proj, off).reshape(m, OUT_B)
        win_new = []
        for gi, (win, _) in enumerate(DIL_GROUPS):
            rows = min(win, t)
            kg = proj[:, t - rows:, kb0 + gi * OUT_B:kb0 + (gi + 1) * OUT_B].reshape(b, rows, 1, H_G, HD_B)
            vg = proj[:, t - rows:, vb0 + gi * OUT_B:vb0 + (gi + 1) * OUT_B].reshape(b, rows, 1, H_G, HD_B)
            win_new.append(jnp.concatenate([kg, vg], axis=2))
    else:
        out_b, win_new = _attn_sample(proj, off, caches, windows, l)
        out_b = out_b.reshape(m, OUT_B)

    hist16 = jnp.concatenate([jnp.zeros((b, 1, C_POOL), F32), pool_hist], axis=1)
    out_c = _pool(proj, off, hist16, w_pool, pool_scale, l, 0 if prompt else SAMPLE_POS0, 256)
    uc0 = off["uc"]
    pool_new = jnp.concatenate([pool_hist, proj[:, :, uc0:uc0 + C_POOL]], axis=1)[:, -POOL_HIST:]

    merged = _merge(out_a.reshape(m, QK_A), out_b, out_c.reshape(m, C_POOL), w_a, w_b, w_c, proj2, off, l, tm, 512)
    x2 = _mm_norm_res(merged, w_out, x2, g_post_mix, l, tm, d)
    act = _ffn_up(x2, g_pre_ffn, w_gu, l, tm, 512)
    d_ff = act.shape[1]
    x2 = _mm_norm_res(act, w_down, x2, g_post_ffn, l, tm, d_ff // 4 if d_ff % (4 * LANE) == 0 else 512)
    return x2.reshape(b, t, d), (win_new[0], win_new[1], win_new[2], s_new, conv_new, pool_new)


def kernel(x_prompt, x_sample, cache_win1, cache_win2, cache_win3, state_gdn, state_conv, state_pool,
           w_in, conv_w, a_log, dt_bias, gdn_gain, w_pool, pool_scale, w_br_a, w_br_b, w_br_c,
           w_out, w_gu, w_down, g_pre_mix, g_post_mix, g_pre_ffn, g_post_ffn):
    depth = w_in.shape[0]
    bp = x_prompt.shape[0]
    d = x_prompt.shape[-1]

    def row(v):
        return v.reshape(depth, 1, v.shape[-1])

    wts = (_prep_w_in(w_in.astype(BF16), d), conv_w, a_log, dt_bias, row(gdn_gain),
           w_pool.astype(BF16), row(pool_scale),
           w_br_a.astype(BF16), w_br_b.astype(BF16), w_br_c.astype(BF16),
           w_out.astype(BF16), w_gu.astype(BF16), w_down.astype(BF16),
           row(g_pre_mix), row(g_post_mix), row(g_pre_ffn), row(g_post_ffn))
    yp, ys = x_prompt, x_sample
    caches = (cache_win1, cache_win2, cache_win3)
    windows = tuple(_shift_cache(c, x_sample.shape[1]) for c in caches)
    new_p, new_s = [], []
    for l in range(depth):
        yp, st_p = _layer(yp, wts, l,
                          jnp.zeros((bp, CONV_K - 1, 3 * QK_A), F32),
                          jnp.zeros((bp, H_A, DK_A, DK_A), F32),
                          jnp.zeros((bp, POOL_HIST, C_POOL), F32),
                          None, None, True)
        ys, st_s = _layer(ys, wts, l, state_conv[l], state_gdn[l], state_pool[l],
                          caches, windows, False)
        windows = st_s[:3]
        new_p.append(st_p)
        new_s.append(st_s)
    outs_p = [jnp.stack([st[i] for st in new_p], axis=0) for i in range(6)]
    outs_s = [jnp.stack([st[i] for st in new_s], axis=0) for i in range(3, 6)]
    return (yp, ys, *outs_p, *windows, *outs_s)
```

```python
import functools

import jax
import jax.numpy as jnp
from jax import lax
from jax.experimental import pallas as pl
from jax.experimental.pallas import tpu as pltpu

F32 = jnp.float32
BF16 = jnp.bfloat16

H_A = 8
DK_A = 128
CONV_K = 4
DIL_GROUPS = ((128, 1), (512, 4), (2048, 16))
N_GROUPS = len(DIL_GROUPS)
H_G = 4
HD_B = 128
N_BACK = 128
POOL_WINDOWS = (2, 4, 8, 16)
CG = 256
C_POOL = CG * len(POOL_WINDOWS)
POOL_HIST = max(POOL_WINDOWS) - 1
SAMPLE_POS0 = 8192
EPS = 1e-6
L2_EPS = 1e-6

LANE = 128
SUBLANE = 8
VMEM_LIMIT = 48 * 1024 * 1024

QK_A = H_A * DK_A
QKV_B = N_GROUPS * H_G * HD_B
OUT_B = H_G * HD_B
GDN_TILE = 256
Q_TILE = 128
ATTN_SEGMENTS = 4
WRONG_RESIDUE = -(1 << 24)


def _proj_layout(d_model):
    off = {}
    c = 0
    for name, width in (("qb", QKV_B), ("kb", QKV_B), ("vb", QKV_B),
                        ("ba", H_A), ("aa", H_A), ("pad0", 512 - 2 * H_A),
                        ("uc", C_POOL),
                        ("qa", QK_A), ("ka", QK_A), ("va", QK_A), ("za", QK_A),
                        ("ga", d_model), ("gb", d_model), ("gc", d_model)):
        off[name] = c
        c += width
    total = -(-c // 512) * 512
    return off, total


def _cparams(sem):
    return pltpu.CompilerParams(dimension_semantics=sem, vmem_limit_bytes=VMEM_LIMIT)


def _dot(a, b):
    return jnp.dot(a.astype(BF16), b.astype(BF16), preferred_element_type=F32)


def _dot_nt(a, b):
    return lax.dot_general(a.astype(BF16), b.astype(BF16), (((1,), (1,)), ((), ())),
                           preferred_element_type=F32)


def _dot_tn(a, b):
    return lax.dot_general(a.astype(BF16), b.astype(BF16), (((0,), (0,)), ((), ())),
                           preferred_element_type=F32)


def _split3(x):
    x1 = x.astype(BF16).astype(F32)
    r1 = x - x1
    x2 = r1.astype(BF16).astype(F32)
    x3 = (r1 - x2).astype(BF16).astype(F32)
    return x1, x2, x3


def _rms_scale(y):
    return lax.rsqrt(jnp.mean(y * y, axis=-1, keepdims=True) + EPS)


def _silu(x):
    return x * jax.nn.sigmoid(x)


def _wspec(l, rows, cols, index):
    return pl.BlockSpec((None, rows, cols), lambda *g: (l,) + index(*g))


def _in_proj_tile_rows(d_model, tn):
    src = {}
    c = 0
    for name, width in (("qa", QK_A), ("ka", QK_A), ("va", QK_A), ("za", QK_A), ("ba", H_A), ("aa", H_A),
                        ("qb", QKV_B), ("kb", QKV_B), ("vb", QKV_B), ("uc", C_POOL),
                        ("ga", d_model), ("gb", d_model), ("gc", d_model)):
        src[name] = c
        c += width
    off, total = _proj_layout(d_model)
    names = [n for n in sorted(off, key=off.get) if n in src]
    rows = []
    for j in range(total // tn):
        owner = max((n for n in names if off[n] <= j * tn), key=off.get)
        row0 = src[owner] + j * tn - off[owner]
        assert row0 % 16 == 0 and row0 + tn <= c
        for n in names:
            if j * tn <= off[n] < (j + 1) * tn:
                assert src[n] - row0 == off[n] - j * tn
        rows.append(row0)
    return rows


def _in_proj_body(rows_ref, x_ref, g_ref, w_hbm, o_ref, h_ref, wbuf, sem, *, l, tn, nj, n_steps):
    j = pl.program_id(1)
    step = pl.program_id(0) * nj + j
    slot = lax.rem(step, 2)

    def fetch(tile, slot_):
        row0 = pl.multiple_of(rows_ref[tile], 16)
        return pltpu.make_async_copy(w_hbm.at[l, pl.ds(row0, tn), :], wbuf.at[slot_], sem.at[slot_])

    @pl.when(step == 0)
    def _():
        fetch(0, 0).start()

    @pl.when(step + 1 < n_steps)
    def _():
        fetch(jnp.where(j + 1 == nj, 0, j + 1), 1 - slot).start()

    @pl.when(j == 0)
    def _():
        xf = x_ref[...]
        h_ref[...] = (xf * _rms_scale(xf) * g_ref[...]).astype(BF16)

    fetch(j, slot).wait()
    o_ref[...] = lax.dot_general(h_ref[...], wbuf[slot], (((1,), (1,)), ((), ())), preferred_element_type=F32)


def _norm_matmul(x, gain, w_t, l, tm, tn):
    m, d = x.shape
    _, n = _proj_layout(d)
    rows = jnp.asarray(_in_proj_tile_rows(d, tn), jnp.int32)
    ni, nj = m // tm, n // tn
    return pl.pallas_call(
        functools.partial(_in_proj_body, l=l, tn=tn, nj=nj, n_steps=ni * nj),
        grid=(ni, nj),
        in_specs=[pl.BlockSpec(memory_space=pltpu.SMEM),
                  pl.BlockSpec((tm, d), lambda i, j: (i, 0)),
                  _wspec(l, 1, d, lambda i, j: (0, 0)),
                  pl.BlockSpec(memory_space=pl.ANY)],
        out_specs=pl.BlockSpec((tm, tn), lambda i, j: (i, j)),
        out_shape=jax.ShapeDtypeStruct((m, n), F32),
        scratch_shapes=[pltpu.VMEM((tm, d), BF16),
                        pltpu.VMEM((2, tn, d), BF16),
                        pltpu.SemaphoreType.DMA((2,))],
        compiler_params=_cparams(("arbitrary", "arbitrary")),
        name="in_proj",
    )(rows, x, gain, w_t)


def _ffn_up_body(x_ref, g_ref, wg_ref, wu_ref, o_ref, h_ref):
    @pl.when(pl.program_id(1) == 0)
    def _():
        xf = x_ref[...]
        h_ref[...] = (xf * _rms_scale(xf) * g_ref[...]).astype(BF16)

    h = h_ref[...]
    gate = jnp.dot(h, wg_ref[...], preferred_element_type=F32)
    up = jnp.dot(h, wu_ref[...], preferred_element_type=F32)
    o_ref[...] = (_silu(gate) * up).astype(BF16)


def _ffn_up(x, gain, w_gu, l, tm, tn):
    m, d = x.shape
    d_ff = w_gu.shape[-1] // 2
    nj = d_ff // tn
    return pl.pallas_call(
        _ffn_up_body,
        grid=(m // tm, nj),
        in_specs=[pl.BlockSpec((tm, d), lambda i, j: (i, 0)),
                  _wspec(l, 1, d, lambda i, j: (0, 0)),
                  _wspec(l, d, tn, lambda i, j: (0, j)),
                  _wspec(l, d, tn, lambda i, j: (0, j + nj))],
        out_specs=pl.BlockSpec((tm, tn), lambda i, j: (i, j)),
        out_shape=jax.ShapeDtypeStruct((m, d_ff), BF16),
        scratch_shapes=[pltpu.VMEM((tm, d), BF16)],
        compiler_params=_cparams(("parallel", "arbitrary")),
        name="ffn_up",
    )(x, gain, w_gu, w_gu)


def _mm_norm_res_body(a_ref, w_ref, res_ref, g_ref, o_ref, acc_ref, *, nk):
    k = pl.program_id(1)

    @pl.when(k == 0)
    def _():
        acc_ref[...] = jnp.zeros_like(acc_ref)

    acc_ref[...] += jnp.dot(a_ref[...], w_ref[...], preferred_element_type=F32)

    @pl.when(k == nk - 1)
    def _():
        y = acc_ref[...]
        o_ref[...] = res_ref[...] + y * _rms_scale(y) * g_ref[...]


def _mm_norm_res_single_body(a_ref, w_ref, res_ref, g_ref, o_ref):
    y = jnp.dot(a_ref[...], w_ref[...], preferred_element_type=F32)
    o_ref[...] = res_ref[...] + y * _rms_scale(y) * g_ref[...]


def _mm_norm_res(a, w, res, gain, l, tm, tk):
    m, kdim = a.shape
    d = w.shape[-1]
    nk = kdim // tk
    assert nk * tk == kdim
    if nk == 1:
        return pl.pallas_call(
            _mm_norm_res_single_body,
            grid=(m // tm,),
            in_specs=[pl.BlockSpec((tm, kdim), lambda i: (i, 0)),
                      _wspec(l, kdim, d, lambda i: (0, 0)),
                      pl.BlockSpec((tm, d), lambda i: (i, 0)),
                      _wspec(l, 1, d, lambda i: (0, 0))],
            out_specs=pl.BlockSpec((tm, d), lambda i: (i, 0)),
            out_shape=jax.ShapeDtypeStruct((m, d), F32),
            compiler_params=_cparams(("parallel",)),
            name="mm_norm_res",
        )(a, w, res, gain)
    return pl.pallas_call(
        functools.partial(_mm_norm_res_body, nk=nk),
        grid=(m // tm, nk),
        in_specs=[pl.BlockSpec((tm, tk), lambda i, k: (i, k)),
                  _wspec(l, tk, d, lambda i, k: (k, 0)),
                  pl.BlockSpec((tm, d), lambda i, k: (i, 0)),
                  _wspec(l, 1, d, lambda i, k: (0, 0))],
        out_specs=pl.BlockSpec((tm, d), lambda i, k: (i, 0)),
        out_shape=jax.ShapeDtypeStruct((m, d), F32),
        scratch_shapes=[pltpu.VMEM((tm, d), F32)],
        compiler_params=_cparams(("parallel", "arbitrary")),
        name="mm_norm_res",
    )(a, w, res, gain)


def _merge_body(a_ref, b_ref, c_ref, wa_ref, wb_ref, wc_ref, ga_ref, gb_ref, gc_ref, o_ref):
    m = jax.nn.sigmoid(ga_ref[...]) * jnp.dot(a_ref[...], wa_ref[...], preferred_element_type=F32)
    m = m + jax.nn.sigmoid(gb_ref[...]) * jnp.dot(b_ref[...], wb_ref[...], preferred_element_type=F32)
    m = m + jax.nn.sigmoid(gc_ref[...]) * jnp.dot(c_ref[...], wc_ref[...], preferred_element_type=F32)
    o_ref[...] = m.astype(BF16)


def _merge(out_a, out_b, out_c, w_a, w_b, w_c, proj2d, off, l, tm, tn):
    m = out_a.shape[0]
    d = w_a.shape[-1]
    ja, jb, jc = off["ga"] // tn, off["gb"] // tn, off["gc"] // tn
    return pl.pallas_call(
        _merge_body,
        grid=(m // tm, d // tn),
        in_specs=[pl.BlockSpec((tm, out_a.shape[1]), lambda i, j: (i, 0)),
                  pl.BlockSpec((tm, out_b.shape[1]), lambda i, j: (i, 0)),
                  pl.BlockSpec((tm, out_c.shape[1]), lambda i, j: (i, 0)),
                  _wspec(l, w_a.shape[1], tn, lambda i, j: (0, j)),
                  _wspec(l, w_b.shape[1], tn, lambda i, j: (0, j)),
                  _wspec(l, w_c.shape[1], tn, lambda i, j: (0, j)),
                  pl.BlockSpec((tm, tn), lambda i, j: (i, j + ja)),
                  pl.BlockSpec((tm, tn), lambda i, j: (i, j + jb)),
                  pl.BlockSpec((tm, tn), lambda i, j: (i, j + jc))],
        out_specs=pl.BlockSpec((tm, tn), lambda i, j: (i, j)),
        out_shape=jax.ShapeDtypeStruct((m, d), BF16),
        compiler_params=_cparams(("parallel", "arbitrary")),
        name="merge",
    )(out_a, out_b, out_c, w_a, w_b, w_c, proj2d, proj2d, proj2d)


def _unit_lower_inverse(lower, c):
    n = lower.shape[0]
    row = lax.broadcasted_iota(jnp.int32, (n, n), 0)
    col = lax.broadcasted_iota(jnp.int32, (n, n), 1)

    def same_block(shift):
        return (row >> shift) == (col >> shift)

    eye = jnp.where(row == col, 1.0, 0.0).astype(F32)
    d1 = jnp.where(same_block(3), lower, 0.0)
    d2 = _dot(d1, d1)
    d4 = _dot(d2, d2)
    x = eye - d1
    x = x + _dot(d2, x)
    x = x + _dot(d4, x)
    shift = 3
    while (1 << shift) < c:
        off_diag = jnp.where(jnp.logical_and(same_block(shift + 1), jnp.logical_not(same_block(shift))),
                             lower, 0.0)
        x = x - _dot(_dot(x, off_diag), x)
        shift += 1
    return x


def _unit_lower_inverses(lowers, c):
    n = lowers[0].shape[0]
    row = lax.broadcasted_iota(jnp.int32, (n, n), 0)
    col = lax.broadcasted_iota(jnp.int32, (n, n), 1)

    def same_block(shift):
        return (row >> shift) == (col >> shift)

    eye = jnp.where(row == col, 1.0, 0.0).astype(F32)
    blk = same_block(3)
    d1 = [jnp.where(blk, lo, 0.0) for lo in lowers]
    d2 = [_dot(d, d) for d in d1]
    d4 = [_dot(d, d) for d in d2]
    xs = [eye - d for d in d1]
    xs = [x + _dot(d, x) for d, x in zip(d2, xs)]
    xs = [x + _dot(d, x) for d, x in zip(d4, xs)]
    shift = 3
    while (1 << shift) < c:
        pick = jnp.logical_and(same_block(shift + 1), jnp.logical_not(same_block(shift)))
        offs = [jnp.where(pick, lo, 0.0) for lo in lowers]
        ts = [_dot(x, o) for x, o in zip(xs, offs)]
        xs = [x - _dot(t, x) for t, x in zip(ts, xs)]
        shift += 1
    return xs


def _gdn_prompt_body(alog_ref, dtb_ref, q_ref, k_ref, v_ref, z_ref, ba_ref, cwq_ref, cwk_ref, cwv_ref,
                     hq_ref, hk_ref, hv_ref, s0_ref, gain_ref, o_ref, sout_ref,
                     s_scr, ext_scr, *, tt, n_tiles, hb):
    c = tt
    h0 = pl.program_id(1) * hb
    it = pl.program_id(2)
    heads = range(hb)

    @pl.when(it == 0)
    def _():
        s_scr[...] = s0_ref[0]
        ext_scr[0, 0:SUBLANE, :] = hq_ref[0]
        ext_scr[1, 0:SUBLANE, :] = hk_ref[0]
        ext_scr[2, 0:SUBLANE, :] = hv_ref[0]

    ext_scr[0, SUBLANE:, :] = q_ref[0]
    ext_scr[1, SUBLANE:, :] = k_ref[0]
    ext_scr[2, SUBLANE:, :] = v_ref[0]

    def conv_act(i, cw_ref):
        base = SUBLANE - (CONV_K - 1)
        y = ext_scr[i, base:base + tt, :] * cw_ref[0:1, :]
        for j in range(1, CONV_K):
            y = y + ext_scr[i, base + j:base + j + tt, :] * cw_ref[j:j + 1, :]
        return _silu(y)

    q_all = conv_act(0, cwq_ref)
    k_all = conv_act(1, cwk_ref)
    v_all = conv_act(2, cwv_ref)

    if n_tiles > 1:
        for i in range(3):
            ext_scr[i, 0:SUBLANE, :] = ext_scr[i, tt:tt + SUBLANE, :]

    ba = ba_ref[0]
    lane = lax.broadcasted_iota(jnp.int32, (tt, LANE), 1)
    lane_c = lax.broadcasted_iota(jnp.int32, (c, LANE), 1)
    row = lax.broadcasted_iota(jnp.int32, (c, c), 0)
    col = lax.broadcasted_iota(jnp.int32, (c, c), 1)
    incl = row >= col
    strict = row > col
    eye = row == col
    tri = jnp.where(incl, 1.0, 0.0).astype(BF16)

    def hsl(hh):
        return slice(hh * LANE, (hh + 1) * LANE)

    qs, ks, vs, betas, g_terms = [], [], [], [], []
    for hh in heads:
        q = q_all[:, hsl(hh)]
        k = k_all[:, hsl(hh)]
        qs.append(q * lax.rsqrt(jnp.sum(q * q, axis=-1, keepdims=True) + L2_EPS) * (DK_A ** -0.5))
        ks.append(k * lax.rsqrt(jnp.sum(k * k, axis=-1, keepdims=True) + L2_EPS))
        vs.append(v_all[:, hsl(hh)])
        b_raw = jnp.sum(jnp.where(lane == h0 + hh, ba, 0.0), axis=1, keepdims=True)
        a_raw = jnp.sum(jnp.where(lane == h0 + hh + H_A, ba, 0.0), axis=1, keepdims=True)
        betas.append(jax.nn.sigmoid(b_raw))
        a_log = jnp.full((1, 1), alog_ref[h0 + hh], F32)
        xs = a_raw + dtb_ref[h0 + hh]
        softplus = jnp.maximum(xs, 0.0) + jnp.log1p(jnp.exp(-jnp.abs(xs)))
        g = -jnp.exp(a_log) * softplus
        g1, g2, g3 = _split3(jnp.broadcast_to(g, (c, LANE)))
        g_terms.append(jnp.where(lane_c == 0, g1, jnp.where(lane_c == 1, g2, g3)).astype(BF16))

    parts = [jnp.dot(tri, gt, preferred_element_type=F32) for gt in g_terms]
    kbs = [k * b for k, b in zip(ks, betas)]
    kqs = [_dot_nt(jnp.concatenate([kb, q], axis=0), k) for kb, q, k in zip(kbs, qs, ks)]
    gc_cols, decays = [], []
    for part in parts:
        gc_col = part[:, 0:1] + part[:, 1:2] + part[:, 2:3]
        gc_cb = jnp.broadcast_to(gc_col, (c, c))
        gc_rb = jnp.sum(jnp.where(eye, gc_cb, 0.0), axis=0, keepdims=True)
        decays.append(jnp.where(incl, jnp.exp(jnp.where(incl, gc_cb - gc_rb, 0.0)), 0.0))
        gc_cols.append(gc_col)
    lowers = [jnp.where(strict, kq[0:c] * dec, 0.0) for kq, dec in zip(kqs, decays)]
    a_intras = [jnp.where(incl, kq[c:2 * c] * dec, 0.0) for kq, dec in zip(kqs, decays)]
    tinvs = _unit_lower_inverses(lowers, c)
    egcs = [jnp.exp(gc) for gc in gc_cols]
    uws = [_dot(tinv, jnp.concatenate([v * b, kb * egc], axis=1))
           for tinv, v, b, kb, egc in zip(tinvs, vs, betas, kbs, egcs)]
    s_old = [s_scr[hh] for hh in heads]
    ws_qs = [_dot(jnp.concatenate([uw[:, DK_A:2 * DK_A], q * egc], axis=0), s)
             for uw, q, egc, s in zip(uws, qs, egcs, s_old)]
    v_news = [uw[:, 0:DK_A] - wq[0:c] for uw, wq in zip(uws, ws_qs)]
    outs = [wq[c:2 * c] + _dot(a, vn) for wq, a, vn in zip(ws_qs, a_intras, v_news)]
    for hh in heads:
        g_last = gc_cols[hh][c - 1:c, :]
        kd = ks[hh] * jnp.exp(g_last - gc_cols[hh])
        s_scr[hh] = s_old[hh] * jnp.exp(g_last) + _dot_tn(kd, v_news[hh])
    z_all = z_ref[0]
    for hh in heads:
        o = outs[hh]
        o = o * _rms_scale(o) * gain_ref[...]
        o = o * _silu(z_all[:, hsl(hh)])
        o_ref[0, :, hsl(hh)] = o.astype(o_ref.dtype)

    @pl.when(it == n_tiles - 1)
    def _():
        sout_ref[0] = s_scr[...]


def _gdn(proj, off, conv_w, conv_hist8, s0, a_log, dt_bias, gdn_gain, l, *, tile, hb):
    b, t, _ = proj.shape
    assert t % tile == 0
    tt = tr = tile
    n_tiles = t // tile
    wid = hb * LANE
    cq, ck, cv, cz = (off[n] // wid for n in ("qa", "ka", "va", "za"))
    cba = off["ba"] // LANE
    nhb = H_A // hb

    def tok(col0):
        return pl.BlockSpec((1, tr, wid), lambda bi, hi, ti: (bi, ti, col0 + hi))

    def per_head(col0, rows):
        return _wspec(l, rows, wid, lambda bi, hi, ti: (0, col0 + hi))

    def hist(col0):
        return pl.BlockSpec((1, SUBLANE, wid), lambda bi, hi, ti: (bi, 0, col0 + hi))

    smem = pl.BlockSpec(memory_space=pltpu.SMEM)
    state_spec = pl.BlockSpec((1, hb, DK_A, DK_A), lambda bi, hi, ti: (bi, hi, 0, 0))
    body = functools.partial(_gdn_prompt_body, tt=tt, n_tiles=n_tiles, hb=hb)
    return pl.pallas_call(
        body,
        grid=(b, nhb, n_tiles),
        in_specs=[smem, smem,
                  tok(cq), tok(ck), tok(cv), tok(cz),
                  pl.BlockSpec((1, tr, LANE), lambda bi, hi, ti: (bi, ti, cba)),
                  per_head(0, CONV_K), per_head(nhb, CONV_K), per_head(2 * nhb, CONV_K),
                  hist(0), hist(nhb), hist(2 * nhb),
                  state_spec,
                  _wspec(l, 1, DK_A, lambda bi, hi, ti: (0, 0))],
        out_specs=[pl.BlockSpec((1, tr, wid), lambda bi, hi, ti: (bi, ti, hi)),
                   state_spec],
        out_shape=[jax.ShapeDtypeStruct((b, t, QK_A), BF16),
                   jax.ShapeDtypeStruct((b, H_A, DK_A, DK_A), F32)],
        scratch_shapes=[pltpu.VMEM((hb, DK_A, DK_A), F32),
                        pltpu.VMEM((3, tt + SUBLANE, wid), F32)],
        compiler_params=_cparams(("parallel", "parallel", "arbitrary")),
        name="gdn",
    )(a_log[l], dt_bias[l], proj, proj, proj, proj, proj, conv_w, conv_w, conv_w,
      conv_hist8, conv_hist8, conv_hist8, s0, gdn_gain)


def _gdn_sample_body(alog_ref, dtb_ref, q_ref, k_ref, v_ref, z_ref, ba_ref, cwq_ref, cwk_ref, cwv_ref,
                     hq_ref, hk_ref, hv_ref, s0_ref, gain_ref, o_ref, sout_ref, ext_scr, z_scr, ba_scr, *, tr):
    tp = SUBLANE
    n = H_A * tp
    for i, (x_ref, h_ref) in enumerate(((q_ref, hq_ref), (k_ref, hk_ref), (v_ref, hv_ref))):
        ext_scr[i, 0:SUBLANE, :] = h_ref[0]
        ext_scr[i, SUBLANE:, :] = jnp.zeros((tp, QK_A), F32)
        ext_scr[i, SUBLANE:SUBLANE + tr, :] = x_ref[0]

    def conv_act(i, cw_ref):
        base = SUBLANE - (CONV_K - 1)
        y = ext_scr[i, base:base + tp, :] * cw_ref[0:1, :]
        for j in range(1, CONV_K):
            y = y + ext_scr[i, base + j:base + j + tp, :] * cw_ref[j:j + 1, :]
        return _silu(y)

    def pack(x):
        return jnp.concatenate([x[:, h * LANE:(h + 1) * LANE] for h in range(H_A)], axis=0)

    q = pack(conv_act(0, cwq_ref))
    k = pack(conv_act(1, cwk_ref))
    v = pack(conv_act(2, cwv_ref))
    q = q * lax.rsqrt(jnp.sum(q * q, axis=-1, keepdims=True) + L2_EPS) * (DK_A ** -0.5)
    k = k * lax.rsqrt(jnp.sum(k * k, axis=-1, keepdims=True) + L2_EPS)

    ba_scr[...] = jnp.zeros((tp, LANE), F32)
    ba_scr[0:tr, :] = ba_ref[0]
    ba = ba_scr[...]
    b_raw = jnp.concatenate([ba[:, h:h + 1] for h in range(H_A)], axis=0)
    a_raw = jnp.concatenate([ba[:, H_A + h:H_A + h + 1] for h in range(H_A)], axis=0)
    a_log = jnp.concatenate([jnp.full((tp, 1), alog_ref[h], F32) for h in range(H_A)], axis=0)
    dtb = jnp.concatenate([jnp.full((tp, 1), dtb_ref[h], F32) for h in range(H_A)], axis=0)
    beta = jax.nn.sigmoid(b_raw)
    xs = a_raw + dtb
    g = -jnp.exp(a_log) * (jnp.maximum(xs, 0.0) + jnp.log1p(jnp.exp(-jnp.abs(xs))))

    valid = (lax.broadcasted_iota(jnp.int32, (n, 1), 0) & (tp - 1)) < tr
    q = jnp.where(valid, q, 0.0)
    k = jnp.where(valid, k, 0.0)
    v = jnp.where(valid, v, 0.0)
    beta = jnp.where(valid, beta, 0.0)
    g = jnp.where(valid, g, 0.0)

    row = lax.broadcasted_iota(jnp.int32, (n, n), 0)
    col = lax.broadcasted_iota(jnp.int32, (n, n), 1)
    same_head = (row >> 3) == (col >> 3)
    incl = jnp.logical_and(same_head, row >= col)
    strict = jnp.logical_and(same_head, row > col)
    eye = row == col
    sum_mat = jnp.concatenate([jnp.where(incl, 1.0, 0.0), jnp.where(same_head, 1.0, 0.0)], axis=0).astype(BF16)
    g1, g2, g3 = _split3(jnp.broadcast_to(g, (n, LANE)))
    lane_c = lax.broadcasted_iota(jnp.int32, (n, LANE), 1)
    g_terms = jnp.where(lane_c == 0, g1, jnp.where(lane_c == 1, g2, g3))
    part = jnp.dot(sum_mat, g_terms.astype(BF16), preferred_element_type=F32)
    sums = part[:, 0:1] + part[:, 1:2] + part[:, 2:3]
    gc_col = sums[0:n]
    g_last = sums[n:2 * n]
    gc_cb = jnp.broadcast_to(gc_col, (n, n))
    gc_rb = jnp.sum(jnp.where(eye, gc_cb, 0.0), axis=0, keepdims=True)
    decay = jnp.where(incl, jnp.exp(jnp.where(incl, gc_cb - gc_rb, 0.0)), 0.0)
    kb = k * beta
    kq = _dot_nt(jnp.concatenate([kb, q], axis=0), k)
    lower = jnp.where(strict, kq[0:n] * decay, 0.0)
    a_intra = jnp.where(incl, kq[n:2 * n] * decay, 0.0)
    tinv = _unit_lower_inverse(lower, tp)
    egc = jnp.exp(gc_col)
    uw = _dot(tinv, jnp.concatenate([v * beta, kb * egc], axis=1))
    kd = k * jnp.exp(g_last - gc_col)

    row_e = lax.broadcasted_iota(jnp.int32, (n, QK_A), 0)
    col_e = lax.broadcasted_iota(jnp.int32, (n, QK_A), 1)
    own = (row_e >> 3) == (col_e >> 7)

    def expand(x):
        return jnp.where(own, jnp.concatenate([x] * H_A, axis=1), 0.0)

    s = s0_ref[0].reshape(H_A * DK_A, DK_A)
    ws_qs = _dot(jnp.concatenate([expand(uw[:, DK_A:2 * DK_A]), expand(q * egc)], axis=0), s)
    v_new = uw[:, 0:DK_A] - ws_qs[0:n]
    o = ws_qs[n:2 * n] + _dot(a_intra, v_new)
    upd = _dot_tn(expand(kd), v_new)
    for h in range(H_A):
        eg = jnp.exp(g_last[h * tp:h * tp + 1, :])
        sout_ref[0, h] = s[h * DK_A:(h + 1) * DK_A] * eg + upd[h * DK_A:(h + 1) * DK_A]

    o = o * _rms_scale(o) * gain_ref[...]
    z_scr[...] = jnp.zeros((tp, QK_A), F32)
    z_scr[0:tr, :] = z_ref[0]
    o = o * _silu(pack(z_scr[...]))
    o_tok = jnp.concatenate([o[h * tp:(h + 1) * tp] for h in range(H_A)], axis=1)
    o_ref[0] = o_tok[0:tr].astype(o_ref.dtype)


def _gdn_sample(proj, off, conv_w, conv_hist8, s0, a_log, dt_bias, gdn_gain, l):
    b, t, _ = proj.shape
    assert t <= SUBLANE
    cq, ck, cv, cz = (off[n] // QK_A for n in ("qa", "ka", "va", "za"))
    cba = off["ba"] // LANE

    def tok(col0):
        return pl.BlockSpec((1, t, QK_A), lambda bi: (bi, 0, col0))

    def conv(col0):
        return _wspec(l, CONV_K, QK_A, lambda bi: (0, col0))

    def hist(col0):
        return pl.BlockSpec((1, SUBLANE, QK_A), lambda bi: (bi, 0, col0))

    smem = pl.BlockSpec(memory_space=pltpu.SMEM)
    state_spec = pl.BlockSpec((1, H_A, DK_A, DK_A), lambda bi: (bi, 0, 0, 0))
    return pl.pallas_call(
        functools.partial(_gdn_sample_body, tr=t),
        grid=(b,),
        in_specs=[smem, smem, tok(cq), tok(ck), tok(cv), tok(cz),
                  pl.BlockSpec((1, t, LANE), lambda bi: (bi, 0, cba)),
                  conv(0), conv(1), conv(2), hist(0), hist(1), hist(2),
                  state_spec, _wspec(l, 1, DK_A, lambda bi: (0, 0))],
        out_specs=[pl.BlockSpec((1, t, QK_A), lambda bi: (bi, 0, 0)), state_spec],
        out_shape=[jax.ShapeDtypeStruct((b, t, QK_A), BF16),
                   jax.ShapeDtypeStruct((b, H_A, DK_A, DK_A), F32)],
        scratch_shapes=[pltpu.VMEM((3, 2 * SUBLANE, QK_A), F32),
                        pltpu.VMEM((SUBLANE, QK_A), F32),
                        pltpu.VMEM((SUBLANE, LANE), F32)],
        compiler_params=_cparams(("parallel",)),
        name="gdn_sample",
    )(a_log[l], dt_bias[l], proj, proj, proj, proj, proj, conv_w, conv_w, conv_w,
      conv_hist8, conv_hist8, conv_hist8, s0, gdn_gain)


def _attn_body(q0_ref, q1_ref, q2_ref, k0_ref, k1_ref, k2_ref, v0_ref, v1_ref, v2_ref, o_ref,
               kb_scr, vb_scr, d0_scr, d1_scr, d2_scr, *, s_len):
    q_refs = (q0_ref, q1_ref, q2_ref)
    k_refs = (k0_ref, k1_ref, k2_ref)
    v_refs = (v0_ref, v1_ref, v2_ref)
    dist_scrs = (d0_scr, d1_scr, d2_scr)
    tq_n = Q_TILE
    scale = HD_B ** -0.5
    for gi, (win, dil) in enumerate(DIL_GROUPS):
        kb_scr[gi] = k_refs[gi][0].astype(BF16)
        vb_scr[gi] = v_refs[gi][0].astype(BF16)
        nk = dist_scrs[gi].shape[1]
        rc = lax.broadcasted_iota(jnp.int32, (tq_n, nk), 0) - lax.broadcasted_iota(jnp.int32, (tq_n, nk), 1)
        dist_scrs[gi][...] = jnp.where((rc & (dil - 1)) == 0, rc, WRONG_RESIDUE).astype(F32)

    def tile(i, carry, cap):
        q0 = pl.multiple_of(i * tq_n, tq_n)
        scores = []
        starts = []
        m = None
        for gi, (win, dil) in enumerate(DIL_GROUPS):
            nk = min(cap, win + tq_n)
            k0 = pl.multiple_of(jnp.clip(q0 - win, 0, cap - nk), tq_n)
            q = q_refs[gi][0, pl.ds(q0, tq_n), :]
            s = _dot_nt(q, kb_scr[gi, pl.ds(k0, nk), :]) * scale
            d = dist_scrs[gi][:, 0:nk] + (q0 - k0).astype(F32)
            ok = d >= 0.0
            if win < s_len - 1:
                ok = jnp.logical_and(ok, d <= float(win))
            s = jnp.where(ok, s, -jnp.inf)
            mg = jnp.max(s, axis=1, keepdims=True)
            m = mg if m is None else jnp.maximum(m, mg)
            scores.append(s)
            starts.append((k0, nk))
        acc = jnp.zeros((tq_n, HD_B), F32)
        den = jnp.zeros((tq_n, 1), F32)
        for gi in range(N_GROUPS):
            k0, nk = starts[gi]
            p = jnp.exp(scores[gi] - m)
            den = den + jnp.sum(p, axis=1, keepdims=True)
            acc = acc + _dot(p, vb_scr[gi, pl.ds(k0, nk), :])
        o_ref[0, pl.ds(q0, tq_n), :] = (acc / den).astype(o_ref.dtype)
        return carry

    n_tiles = s_len // tq_n
    seg = max(1, n_tiles // ATTN_SEGMENTS)
    for lo in range(0, n_tiles, seg):
        hi = min(lo + seg, n_tiles)
        lax.fori_loop(lo, hi, functools.partial(tile, cap=hi * tq_n), 0)


def _attn_prompt(proj, off):
    b, s_len, _ = proj.shape
    cq, ck, cv = (off[n] // LANE for n in ("qb", "kb", "vb"))

    def spec(col0, gi):
        return pl.BlockSpec((1, s_len, LANE), lambda bi, hi: (bi, 0, col0 + gi * H_G + hi))

    return pl.pallas_call(
        functools.partial(_attn_body, s_len=s_len),
        grid=(b, H_G),
        in_specs=[spec(c0, gi) for c0 in (cq, ck, cv) for gi in range(N_GROUPS)],
        out_specs=pl.BlockSpec((1, s_len, LANE), lambda bi, hi: (bi, 0, hi)),
        out_shape=jax.ShapeDtypeStruct((b, s_len, OUT_B), BF16),
        scratch_shapes=[pltpu.VMEM((N_GROUPS, s_len, HD_B), BF16),
                        pltpu.VMEM((N_GROUPS, s_len, HD_B), BF16)]
        + [pltpu.VMEM((Q_TILE, min(s_len, win + Q_TILE)), F32) for (win, _) in DIL_GROUPS],
        compiler_params=_cparams(("parallel", "parallel")),
        name="attn_prompt",
    )(*([proj] * 9))


SHIFT_UNITS = 128


def _shift_body(a_ref, b_ref, o_ref):
    r = a_ref.shape[0]
    o_ref[0:r - 1] = a_ref[1:r]
    o_ref[r - 1:r] = b_ref[...]


def _shift_cache(c, t_new):
    depth, batch, win = c.shape[:3]
    nu = win // t_new
    r = min(SHIFT_UNITS, nu)
    assert win % t_new == 0 and nu % r == 0
    view = c.reshape(depth, batch, nu, t_new, 2, H_G, HD_B)
    tail = (t_new, 2, H_G, HD_B)
    zeros = (0,) * len(tail)
    out = pl.pallas_call(
        _shift_body,
        grid=(depth, batch, nu // r),
        in_specs=[pl.BlockSpec((None, None, r) + tail, lambda l, b, j: (l, b, j) + zeros),
                  pl.BlockSpec((None, None, 1) + tail,
                               lambda l, b, j: (l, b, jnp.minimum((j + 1) * r, nu - 1)) + zeros)],
        out_specs=pl.BlockSpec((None, None, r) + tail, lambda l, b, j: (l, b, j) + zeros),
        out_shape=jax.ShapeDtypeStruct(view.shape, view.dtype),
        compiler_params=_cparams(("parallel", "parallel", "parallel")),
        name="cache_shift",
    )(view, view)
    return out.reshape(c.shape)


def _attn_sample_body(q_ref, kn_ref, vn_ref, c1_ref, c2_ref, c3_ref, w1_in, w2_in, w3_in,
                      o_ref, w1_ref, w2_ref, w3_ref, *, t_new):
    del w1_in, w2_in, w3_in
    scale = HD_B ** -0.5
    q = q_ref[0]
    kn = kn_ref[0]
    vn = vn_ref[0]
    caches = (c1_ref, c2_ref, c3_ref)
    wins = (w1_ref, w2_ref, w3_ref)
    row_c = lax.broadcasted_iota(jnp.int32, (N_BACK, H_G, 1), 0)

    def heads(x, t, gi):
        c0 = gi * OUT_B
        return jnp.concatenate([x[t:t + 1, c0 + h * HD_B:c0 + (h + 1) * HD_B] for h in range(H_G)], axis=0)

    k_new = [[heads(kn, t, gi) for t in range(t_new)] for gi in range(N_GROUPS)]
    v_new = [[heads(vn, t, gi) for t in range(t_new)] for gi in range(N_GROUPS)]
    for gi in range(N_GROUPS):
        for t in range(t_new):
            wins[gi][0, t, 0] = k_new[gi][t]
            wins[gi][0, t, 1] = v_new[gi][t]

    for t in range(t_new):
        parts = []
        for gi in range(N_GROUPS):
            qv = heads(q, t, gi)
            r = 0 if gi == 0 else t
            kc = caches[gi][0, :, r, 0]
            vc = caches[gi][0, :, r, 1]
            s_c = jnp.sum(kc * qv[None], axis=-1, keepdims=True) * scale
            if gi == 0:
                s_c = jnp.where(row_c >= t, s_c, -jnp.inf)
                news = list(range(t + 1))
            else:
                news = [t]
            s_n = [jnp.sum(k_new[gi][tn] * qv, axis=-1, keepdims=True) * scale for tn in news]
            m = jnp.max(s_c, axis=0)
            for sn in s_n:
                m = jnp.maximum(m, sn)
            p_c = jnp.exp(s_c - m[None])
            l = jnp.sum(p_c, axis=0)
            o = jnp.sum(p_c * vc, axis=0)
            for tn, sn in zip(news, s_n):
                p_n = jnp.exp(sn - m)
                l = l + p_n
                o = o + p_n * v_new[gi][tn]
            parts.append((o, m, l))
        m_all = jnp.maximum(jnp.maximum(parts[0][1], parts[1][1]), parts[2][1])
        num = 0.0
        den = 0.0
        for (o, m, l) in parts:
            e = jnp.exp(m - m_all)
            num = num + e * o
            den = den + e * l
        out = (num / den).astype(o_ref.dtype)
        for h in range(H_G):
            o_ref[0, t:t + 1, h * HD_B:(h + 1) * HD_B] = out[h:h + 1, :]


def _attn_sample(proj, off, caches, windows, l):
    b, t, _ = proj.shape
    depth = caches[0].shape[0]
    assert t <= DIL_GROUPS[1][1] and DIL_GROUPS[0][1] == 1
    views, cspecs = [], []
    for c, (win, dil) in zip(caches, DIL_GROUPS):
        views.append(c.reshape(depth, b, N_BACK, dil, 2, H_G, HD_B))
        rows = min(dil, t)
        cspecs.append(pl.BlockSpec((None, 1, N_BACK, rows, 2, H_G, HD_B),
                                   lambda bi: (l, bi, 0, 0, 0, 0, 0)))
    wspecs = [pl.BlockSpec((None, 1, t, 2, H_G, HD_B),
                           functools.partial(lambda bi, blk: (l, bi, blk, 0, 0, 0), blk=win // t - 1))
              for (win, _) in DIL_GROUPS]

    def pspec(name):
        return pl.BlockSpec((1, t, QKV_B), lambda bi: (bi, 0, off[name] // QKV_B))

    any_spec = pl.BlockSpec(memory_space=pl.ANY)
    res = pl.pallas_call(
        functools.partial(_attn_sample_body, t_new=t),
        grid=(b,),
        in_specs=[pspec("qb"), pspec("kb"), pspec("vb")] + cspecs + [any_spec] * 3,
        out_specs=[pl.BlockSpec((1, t, OUT_B), lambda bi: (bi, 0, 0))] + wspecs,
        out_shape=[jax.ShapeDtypeStruct((b, t, OUT_B), BF16)]
        + [jax.ShapeDtypeStruct(w.shape, w.dtype) for w in windows],
        input_output_aliases={6: 1, 7: 2, 8: 3},
        compiler_params=_cparams(("parallel",)),
        name="attn_sample",
    )(proj, proj, proj, *views, *windows)
    return res[0], tuple(res[1:])


def _pool_body(u_ref, hist_ref, wp_ref, ps_ref, o_ref, ext_scr, *, tt, tr, pos0, n_tiles):
    hp = POOL_HIST + 1
    it = pl.program_id(1)

    @pl.when(it == 0)
    def _():
        ext_scr[0:hp, :] = hist_ref[0]

    if tr < tt:
        ext_scr[hp:, :] = jnp.zeros((tt, C_POOL), F32)
    ext_scr[hp:hp + tr, :] = u_ref[0]

    pos = pos0 + it * tt + lax.broadcasted_iota(jnp.int32, (tt, 1), 0)
    outs = []
    for gi, win in enumerate(POOL_WINDOWS):
        lo, hi = gi * CG, (gi + 1) * CG
        x = ext_scr[hp:hp + tt, lo:hi]
        acc = x
        for back in range(1, win):
            acc = acc + ext_scr[hp - back:hp - back + tt, lo:hi]
        cnt = jnp.minimum(win, pos + 1).astype(F32)
        pooled = acc / cnt - x
        outs.append(jnp.dot(pooled.astype(BF16), wp_ref[gi], preferred_element_type=F32))
    oc = jnp.concatenate(outs, axis=1) * ps_ref[...]
    o_ref[0] = oc[0:tr].astype(o_ref.dtype)

    if n_tiles > 1:
        ext_scr[0:hp, :] = ext_scr[tt:tt + hp, :]


def _pool(proj, off, hist16, w_pool, pool_scale, l, pos0, tile):
    b, t, _ = proj.shape
    hp = POOL_HIST + 1
    if t >= tile:
        tt = tr = tile
        n_tiles = t // tile
    else:
        tt, tr, n_tiles = SUBLANE, t, 1
    cu = off["uc"] // C_POOL
    assert off["uc"] % C_POOL == 0
    ng = len(POOL_WINDOWS)
    return pl.pallas_call(
        functools.partial(_pool_body, tt=tt, tr=tr, pos0=pos0, n_tiles=n_tiles),
        grid=(b, n_tiles),
        in_specs=[pl.BlockSpec((1, tr, C_POOL), lambda bi, ti: (bi, ti, cu)),
                  pl.BlockSpec((1, hp, C_POOL), lambda bi, ti: (bi, 0, 0)),
                  pl.BlockSpec((None, ng, CG, CG), lambda bi, ti: (l, 0, 0, 0)),
                  _wspec(l, 1, C_POOL, lambda bi, ti: (0, 0))],
        out_specs=pl.BlockSpec((1, tr, C_POOL), lambda bi, ti: (bi, ti, 0)),
        out_shape=jax.ShapeDtypeStruct((b, t, C_POOL), BF16),
        scratch_shapes=[pltpu.VMEM((hp + tt, C_POOL), F32)],
        compiler_params=_cparams(("parallel", "arbitrary")),
        name="pool",
    )(proj, hist16, w_pool, pool_scale)


def _tile_rows(m, pref):
    return pref if m % pref == 0 else m


def _layer(x, wts, l, conv_hist, s0, pool_hist, caches, windows, prompt):
    (w_in, conv_w, a_log, dt_bias, gdn_gain, w_pool, pool_scale, w_a, w_b, w_c, w_out, w_gu, w_down,
     g_pre_mix, g_post_mix, g_pre_ffn, g_post_ffn) = wts
    b, t, d = x.shape
    m = b * t
    off, npad = _proj_layout(d)
    x2 = x.reshape(m, d)
    tm = _tile_rows(m, 512)

    proj2 = _norm_matmul(x2, g_pre_mix, w_in, l, _tile_rows(m, 1024), 512)
    proj = proj2.reshape(b, t, npad)

    hist8 = jnp.concatenate([jnp.zeros((b, SUBLANE - (CONV_K - 1), 3 * QK_A), F32), conv_hist], axis=1)
    if prompt:
        out_a, s_new = _gdn(proj, off, conv_w, hist8, s0, a_log, dt_bias, gdn_gain, l, tile=GDN_TILE, hb=4)
    else:
        out_a, s_new = _gdn_sample(proj, off, conv_w, hist8, s0, a_log, dt_bias, gdn_gain, l)
    qa0 = off["qa"]
    conv_new = jnp.concatenate([conv_hist, proj[:, :, qa0:qa0 + 3 * QK_A]], axis=1)[:, -(CONV_K - 1):]

    kb0, vb0 = off["kb"], off["vb"]
    if prompt:
        out_b = _attn_prompt(proj, off).reshape(m, OUT_B)
        win_new = []
        for gi, (win, _) in enumerate(DIL_GROUPS):
            rows = min(win, t)
            kg = proj[:, t - rows:, kb0 + gi * OUT_B:kb0 + (gi + 1) * OUT_B].reshape(b, rows, 1, H_G, HD_B)
            vg = proj[:, t - rows:, vb0 + gi * OUT_B:vb0 + (gi + 1) * OUT_B].reshape(b, rows, 1, H_G, HD_B)
            win_new.append(jnp.concatenate([kg, vg], axis=2))
    else:
        out_b, win_new = _attn_sample(proj, off, caches, windows, l)
        out_b = out_b.reshape(m, OUT_B)

    hist16 = jnp.concatenate([jnp.zeros((b, 1, C_POOL), F32), pool_hist], axis=1)
    out_c = _pool(proj, off, hist16, w_pool, pool_scale, l, 0 if prompt else SAMPLE_POS0, 256)
    uc0 = off["uc"]
    pool_new = jnp.concatenate([pool_hist, proj[:, :, uc0:uc0 + C_POOL]], axis=1)[:, -POOL_HIST:]

    merged = _merge(out_a.reshape(m, QK_A), out_b, out_c.reshape(m, C_POOL), w_a, w_b, w_c, proj2, off, l, tm, 512)
    x2 = _mm_norm_res(merged, w_out, x2, g_post_mix, l, tm, d)
    act = _ffn_up(x2, g_pre_ffn, w_gu, l, tm, 512)
    d_ff = act.shape[1]
    x2 = _mm_norm_res(act, w_down, x2, g_post_ffn, l, tm, d_ff // 4 if d_ff % (4 * LANE) == 0 else 512)
    return x2.reshape(b, t, d), (win_new[0], win_new[1], win_new[2], s_new, conv_new, pool_new)


def kernel(x_prompt, x_sample, cache_win1, cache_win2, cache_win3, state_gdn, state_conv, state_pool,
           w_in, conv_w, a_log, dt_bias, gdn_gain, w_pool, pool_scale, w_br_a, w_br_b, w_br_c,
           w_out, w_gu, w_down, g_pre_mix, g_post_mix, g_pre_ffn, g_post_ffn):
    depth = w_in.shape[0]
    bp = x_prompt.shape[0]
    d = x_prompt.shape[-1]

    def row(v):
        return v.reshape(depth, 1, v.shape[-1])

    wts = (jnp.swapaxes(w_in, 1, 2).astype(BF16), conv_w, a_log, dt_bias, row(gdn_gain),
           w_pool.astype(BF16), row(pool_scale),
           w_br_a.astype(BF16), w_br_b.astype(BF16), w_br_c.astype(BF16),
           w_out.astype(BF16), w_gu.astype(BF16), w_down.astype(BF16),
           row(g_pre_mix), row(g_post_mix), row(g_pre_ffn), row(g_post_ffn))
    yp, ys = x_prompt, x_sample
    caches = (cache_win1, cache_win2, cache_win3)
    windows = tuple(_shift_cache(c, x_sample.shape[1]) for c in caches)
    new_p, new_s = [], []
    for l in range(depth):
        yp, st_p = _layer(yp, wts, l,
                          jnp.zeros((bp, CONV_K - 1, 3 * QK_A), F32),
                          jnp.zeros((bp, H_A, DK_A, DK_A), F32),
                          jnp.zeros((bp, POOL_HIST, C_POOL), F32),
                          None, None, True)
        ys, st_s = _layer(ys, wts, l, state_conv[l], state_gdn[l], state_pool[l],
                          caches, windows, False)
        windows = st_s[:3]
        new_p.append(st_p)
        new_s.append(st_s)
    outs_p = [jnp.stack([st[i] for st in new_p], axis=0) for i in range(6)]
    outs_s = [jnp.stack([st[i] for st in new_s], axis=0) for i in range(3, 6)]
    return (yp, ys, *outs_p, *windows, *outs_s)
```

```python
import functools

import jax
import jax.numpy as jnp
from jax import lax
from jax.experimental import pallas as pl
from jax.experimental.pallas import tpu as pltpu

F32 = jnp.float32
BF16 = jnp.bfloat16

H_A = 8
DK_A = 128
CONV_K = 4
DIL_GROUPS = ((128, 1), (512, 4), (2048, 16))
N_GROUPS = len(DIL_GROUPS)
H_G = 4
HD_B = 128
N_BACK = 128
POOL_WINDOWS = (2, 4, 8, 16)
CG = 256
C_POOL = CG * len(POOL_WINDOWS)
POOL_HIST = max(POOL_WINDOWS) - 1
SAMPLE_POS0 = 8192
EPS = 1e-6
L2_EPS = 1e-6

LANE = 128
SUBLANE = 8
VMEM_LIMIT = 48 * 1024 * 1024

QK_A = H_A * DK_A
QKV_B = N_GROUPS * H_G * HD_B
OUT_B = H_G * HD_B
TM_WIDE = 1024
TM_NORM = 512
TN = 512
POOL_TILE = 256
GDN_TILE = 256
Q_TILE = 128
ATTN_SEGMENTS = 4
WRONG_RESIDUE = -(1 << 24)


def _proj_layout(d_model):
    off = {}
    c = 0
    for name, width in (("qb", QKV_B), ("kb", QKV_B), ("vb", QKV_B),
                        ("ba", H_A), ("aa", H_A), ("pad0", 512 - 2 * H_A),
                        ("uc", C_POOL),
                        ("qa", QK_A), ("ka", QK_A), ("va", QK_A), ("za", QK_A),
                        ("ga", d_model), ("gb", d_model), ("gc", d_model)):
        off[name] = c
        c += width
    total = -(-c // 512) * 512
    return off, total


def _cparams(sem):
    return pltpu.CompilerParams(dimension_semantics=sem, vmem_limit_bytes=VMEM_LIMIT)


def _dot(a, b):
    return jnp.dot(a.astype(BF16), b.astype(BF16), preferred_element_type=F32)


def _dot_nt(a, b):
    return lax.dot_general(a.astype(BF16), b.astype(BF16), (((1,), (1,)), ((), ())),
                           preferred_element_type=F32)


def _dot_tn(a, b):
    return lax.dot_general(a.astype(BF16), b.astype(BF16), (((0,), (0,)), ((), ())),
                           preferred_element_type=F32)


def _split3(x):
    x1 = x.astype(BF16).astype(F32)
    r1 = x - x1
    x2 = r1.astype(BF16).astype(F32)
    x3 = (r1 - x2).astype(BF16).astype(F32)
    return x1, x2, x3


def _rms_scale(y):
    return lax.rsqrt(jnp.mean(y * y, axis=-1, keepdims=True) + EPS)


def _silu(x):
    return x * jax.nn.sigmoid(x)


def _wspec(l, rows, cols, index):
    return pl.BlockSpec((None, rows, cols), lambda *g: (l,) + index(*g))


def _in_proj_tile_rows(d_model, tn):
    src = {}
    c = 0
    for name, width in (("qa", QK_A), ("ka", QK_A), ("va", QK_A), ("za", QK_A), ("ba", H_A), ("aa", H_A),
                        ("qb", QKV_B), ("kb", QKV_B), ("vb", QKV_B), ("uc", C_POOL),
                        ("ga", d_model), ("gb", d_model), ("gc", d_model)):
        src[name] = c
        c += width
    off, total = _proj_layout(d_model)
    names = [n for n in sorted(off, key=off.get) if n in src]
    rows = []
    for j in range(total // tn):
        owner = max((n for n in names if off[n] <= j * tn), key=off.get)
        row0 = src[owner] + j * tn - off[owner]
        assert row0 % 16 == 0 and row0 + tn <= c
        for n in names:
            if j * tn <= off[n] < (j + 1) * tn:
                assert src[n] - row0 == off[n] - j * tn
        rows.append(row0)
    return rows


def _in_proj_body(rows_ref, x_ref, g_ref, w_hbm, o_ref, h_ref, wbuf, sem, *, l, tn, nj, n_steps):
    j = pl.program_id(1)
    step = pl.program_id(0) * nj + j
    slot = lax.rem(step, 2)

    def fetch(tile, slot_):
        row0 = pl.multiple_of(rows_ref[tile], 16)
        return pltpu.make_async_copy(w_hbm.at[l, pl.ds(row0, tn), :], wbuf.at[slot_], sem.at[slot_])

    @pl.when(step == 0)
    def _():
        fetch(0, 0).start()

    @pl.when(step + 1 < n_steps)
    def _():
        fetch(jnp.where(j + 1 == nj, 0, j + 1), 1 - slot).start()

    @pl.when(j == 0)
    def _():
        xf = x_ref[...]
        h_ref[...] = (xf * _rms_scale(xf) * g_ref[...]).astype(BF16)

    fetch(j, slot).wait()
    o_ref[...] = lax.dot_general(h_ref[...], wbuf[slot], (((1,), (1,)), ((), ())), preferred_element_type=F32)


def _norm_matmul(x, gain, w_t, l, tm, tn):
    m, d = x.shape
    _, n = _proj_layout(d)
    rows = jnp.asarray(_in_proj_tile_rows(d, tn), jnp.int32)
    ni, nj = m // tm, n // tn
    return pl.pallas_call(
        functools.partial(_in_proj_body, l=l, tn=tn, nj=nj, n_steps=ni * nj),
        grid=(ni, nj),
        in_specs=[pl.BlockSpec(memory_space=pltpu.SMEM),
                  pl.BlockSpec((tm, d), lambda i, j: (i, 0)),
                  _wspec(l, 1, d, lambda i, j: (0, 0)),
                  pl.BlockSpec(memory_space=pl.ANY)],
        out_specs=pl.BlockSpec((tm, tn), lambda i, j: (i, j)),
        out_shape=jax.ShapeDtypeStruct((m, n), F32),
        scratch_shapes=[pltpu.VMEM((tm, d), BF16),
                        pltpu.VMEM((2, tn, d), BF16),
                        pltpu.SemaphoreType.DMA((2,))],
        compiler_params=_cparams(("arbitrary", "arbitrary")),
        name="in_proj",
    )(rows, x, gain, w_t)


def _ffn_up_body(x_ref, g_ref, wg_ref, wu_ref, o_ref, h_ref):
    @pl.when(pl.program_id(1) == 0)
    def _():
        xf = x_ref[...]
        h_ref[...] = (xf * _rms_scale(xf) * g_ref[...]).astype(BF16)

    h = h_ref[...]
    gate = jnp.dot(h, wg_ref[...], preferred_element_type=F32)
    up = jnp.dot(h, wu_ref[...], preferred_element_type=F32)
    o_ref[...] = (_silu(gate) * up).astype(BF16)


def _ffn_up(x, gain, w_gu, l, tm, tn):
    m, d = x.shape
    d_ff = w_gu.shape[-1] // 2
    nj = d_ff // tn
    return pl.pallas_call(
        _ffn_up_body,
        grid=(m // tm, nj),
        in_specs=[pl.BlockSpec((tm, d), lambda i, j: (i, 0)),
                  _wspec(l, 1, d, lambda i, j: (0, 0)),
                  _wspec(l, d, tn, lambda i, j: (0, j)),
                  _wspec(l, d, tn, lambda i, j: (0, j + nj))],
        out_specs=pl.BlockSpec((tm, tn), lambda i, j: (i, j)),
        out_shape=jax.ShapeDtypeStruct((m, d_ff), BF16),
        scratch_shapes=[pltpu.VMEM((tm, d), BF16)],
        compiler_params=_cparams(("parallel", "arbitrary")),
        name="ffn_up",
    )(x, gain, w_gu, w_gu)


def _mm_norm_res_body(a_ref, w_ref, res_ref, g_ref, o_ref, acc_ref, *, nk):
    k = pl.program_id(1)

    @pl.when(k == 0)
    def _():
        acc_ref[...] = jnp.zeros_like(acc_ref)

    acc_ref[...] += jnp.dot(a_ref[...], w_ref[...], preferred_element_type=F32)

    @pl.when(k == nk - 1)
    def _():
        y = acc_ref[...]
        o_ref[...] = res_ref[...] + y * _rms_scale(y) * g_ref[...]


def _mm_norm_res_single_body(a_ref, w_ref, res_ref, g_ref, o_ref):
    y = jnp.dot(a_ref[...], w_ref[...], preferred_element_type=F32)
    o_ref[...] = res_ref[...] + y * _rms_scale(y) * g_ref[...]


def _mm_norm_res(a, w, res, gain, l, tm, tk):
    m, kdim = a.shape
    d = w.shape[-1]
    nk = kdim // tk
    assert nk * tk == kdim
    if nk == 1:
        return pl.pallas_call(
            _mm_norm_res_single_body,
            grid=(m // tm,),
            in_specs=[pl.BlockSpec((tm, kdim), lambda i: (i, 0)),
                      _wspec(l, kdim, d, lambda i: (0, 0)),
                      pl.BlockSpec((tm, d), lambda i: (i, 0)),
                      _wspec(l, 1, d, lambda i: (0, 0))],
            out_specs=pl.BlockSpec((tm, d), lambda i: (i, 0)),
            out_shape=jax.ShapeDtypeStruct((m, d), F32),
            compiler_params=_cparams(("parallel",)),
            name="mm_norm_res",
        )(a, w, res, gain)
    return pl.pallas_call(
        functools.partial(_mm_norm_res_body, nk=nk),
        grid=(m // tm, nk),
        in_specs=[pl.BlockSpec((tm, tk), lambda i, k: (i, k)),
                  _wspec(l, tk, d, lambda i, k: (k, 0)),
                  pl.BlockSpec((tm, d), lambda i, k: (i, 0)),
                  _wspec(l, 1, d, lambda i, k: (0, 0))],
        out_specs=pl.BlockSpec((tm, d), lambda i, k: (i, 0)),
        out_shape=jax.ShapeDtypeStruct((m, d), F32),
        scratch_shapes=[pltpu.VMEM((tm, d), F32)],
        compiler_params=_cparams(("parallel", "arbitrary")),
        name="mm_norm_res",
    )(a, w, res, gain)


def _merge_body(a_ref, b_ref, c_ref, wa_ref, wb_ref, wc_ref, ga_ref, gb_ref, gc_ref, o_ref):
    m = jax.nn.sigmoid(ga_ref[...]) * jnp.dot(a_ref[...], wa_ref[...], preferred_element_type=F32)
    m = m + jax.nn.sigmoid(gb_ref[...]) * jnp.dot(b_ref[...], wb_ref[...], preferred_element_type=F32)
    m = m + jax.nn.sigmoid(gc_ref[...]) * jnp.dot(c_ref[...], wc_ref[...], preferred_element_type=F32)
    o_ref[...] = m.astype(BF16)


def _merge(out_a, out_b, out_c, w_a, w_b, w_c, proj2d, off, l, tm, tn):
    m = out_a.shape[0]
    d = w_a.shape[-1]
    ja, jb, jc = off["ga"] // tn, off["gb"] // tn, off["gc"] // tn
    return pl.pallas_call(
        _merge_body,
        grid=(m // tm, d // tn),
        in_specs=[pl.BlockSpec((tm, out_a.shape[1]), lambda i, j: (i, 0)),
                  pl.BlockSpec((tm, out_b.shape[1]), lambda i, j: (i, 0)),
                  pl.BlockSpec((tm, out_c.shape[1]), lambda i, j: (i, 0)),
                  _wspec(l, w_a.shape[1], tn, lambda i, j: (0, j)),
                  _wspec(l, w_b.shape[1], tn, lambda i, j: (0, j)),
                  _wspec(l, w_c.shape[1], tn, lambda i, j: (0, j)),
                  pl.BlockSpec((tm, tn), lambda i, j: (i, j + ja)),
                  pl.BlockSpec((tm, tn), lambda i, j: (i, j + jb)),
                  pl.BlockSpec((tm, tn), lambda i, j: (i, j + jc))],
        out_specs=pl.BlockSpec((tm, tn), lambda i, j: (i, j)),
        out_shape=jax.ShapeDtypeStruct((m, d), BF16),
        compiler_params=_cparams(("parallel", "arbitrary")),
        name="merge",
    )(out_a, out_b, out_c, w_a, w_b, w_c, proj2d, proj2d, proj2d)


def _unit_lower_inverse(lower, c):
    n = lower.shape[0]
    row = lax.broadcasted_iota(jnp.int32, (n, n), 0)
    col = lax.broadcasted_iota(jnp.int32, (n, n), 1)

    def same_block(shift):
        return (row >> shift) == (col >> shift)

    eye = jnp.where(row == col, 1.0, 0.0).astype(F32)
    d1 = jnp.where(same_block(3), lower, 0.0)
    d2 = _dot(d1, d1)
    d4 = _dot(d2, d2)
    x = eye - d1
    x = x + _dot(d2, x)
    x = x + _dot(d4, x)
    shift = 3
    while (1 << shift) < c:
        off_diag = jnp.where(jnp.logical_and(same_block(shift + 1), jnp.logical_not(same_block(shift))),
                             lower, 0.0)
        x = x - _dot(_dot(x, off_diag), x)
        shift += 1
    return x


def _unit_lower_inverses(lowers, c):
    n = lowers[0].shape[0]
    row = lax.broadcasted_iota(jnp.int32, (n, n), 0)
    col = lax.broadcasted_iota(jnp.int32, (n, n), 1)

    def same_block(shift):
        return (row >> shift) == (col >> shift)

    eye = jnp.where(row == col, 1.0, 0.0).astype(F32)
    blk = same_block(3)
    d1 = [jnp.where(blk, lo, 0.0) for lo in lowers]
    d2 = [_dot(d, d) for d in d1]
    d4 = [_dot(d, d) for d in d2]
    xs = [eye - d for d in d1]
    xs = [x + _dot(d, x) for d, x in zip(d2, xs)]
    xs = [x + _dot(d, x) for d, x in zip(d4, xs)]
    shift = 3
    while (1 << shift) < c:
        pick = jnp.logical_and(same_block(shift + 1), jnp.logical_not(same_block(shift)))
        offs = [jnp.where(pick, lo, 0.0) for lo in lowers]
        ts = [_dot(x, o) for x, o in zip(xs, offs)]
        xs = [x - _dot(t, x) for t, x in zip(ts, xs)]
        shift += 1
    return xs


def _gdn_prompt_body(alog_ref, dtb_ref, q_ref, k_ref, v_ref, z_ref, ba_ref, cwq_ref, cwk_ref, cwv_ref,
                     hq_ref, hk_ref, hv_ref, s0_ref, gain_ref, o_ref, sout_ref,
                     s_scr, ext_scr, *, tt, n_tiles, hb):
    c = tt
    h0 = pl.program_id(1) * hb
    it = pl.program_id(2)
    heads = range(hb)

    @pl.when(it == 0)
    def _():
        s_scr[...] = s0_ref[0]
        ext_scr[0, 0:SUBLANE, :] = hq_ref[0]
        ext_scr[1, 0:SUBLANE, :] = hk_ref[0]
        ext_scr[2, 0:SUBLANE, :] = hv_ref[0]

    ext_scr[0, SUBLANE:, :] = q_ref[0]
    ext_scr[1, SUBLANE:, :] = k_ref[0]
    ext_scr[2, SUBLANE:, :] = v_ref[0]

    def conv_act(i, cw_ref):
        base = SUBLANE - (CONV_K - 1)
        y = ext_scr[i, base:base + tt, :] * cw_ref[0:1, :]
        for j in range(1, CONV_K):
            y = y + ext_scr[i, base + j:base + j + tt, :] * cw_ref[j:j + 1, :]
        return _silu(y)

    q_all = conv_act(0, cwq_ref)
    k_all = conv_act(1, cwk_ref)
    v_all = conv_act(2, cwv_ref)

    if n_tiles > 1:
        for i in range(3):
            ext_scr[i, 0:SUBLANE, :] = ext_scr[i, tt:tt + SUBLANE, :]

    ba = ba_ref[0]
    lane = lax.broadcasted_iota(jnp.int32, (tt, LANE), 1)
    lane_c = lax.broadcasted_iota(jnp.int32, (c, LANE), 1)
    row = lax.broadcasted_iota(jnp.int32, (c, c), 0)
    col = lax.broadcasted_iota(jnp.int32, (c, c), 1)
    incl = row >= col
    strict = row > col
    eye = row == col
    tri = jnp.where(incl, 1.0, 0.0).astype(BF16)

    def hsl(hh):
        return slice(hh * LANE, (hh + 1) * LANE)

    qs, ks, vs, betas, g_terms = [], [], [], [], []
    for hh in heads:
        q = q_all[:, hsl(hh)]
        k = k_all[:, hsl(hh)]
        qs.append(q * lax.rsqrt(jnp.sum(q * q, axis=-1, keepdims=True) + L2_EPS) * (DK_A ** -0.5))
        ks.append(k * lax.rsqrt(jnp.sum(k * k, axis=-1, keepdims=True) + L2_EPS))
        vs.append(v_all[:, hsl(hh)])
        b_raw = jnp.sum(jnp.where(lane == h0 + hh, ba, 0.0), axis=1, keepdims=True)
        a_raw = jnp.sum(jnp.where(lane == h0 + hh + H_A, ba, 0.0), axis=1, keepdims=True)
        betas.append(jax.nn.sigmoid(b_raw))
        a_log = jnp.full((1, 1), alog_ref[h0 + hh], F32)
        xs = a_raw + dtb_ref[h0 + hh]
        softplus = jnp.maximum(xs, 0.0) + jnp.log1p(jnp.exp(-jnp.abs(xs)))
        g = -jnp.exp(a_log) * softplus
        g1, g2, g3 = _split3(jnp.broadcast_to(g, (c, LANE)))
        g_terms.append(jnp.where(lane_c == 0, g1, jnp.where(lane_c == 1, g2, g3)).astype(BF16))

    parts = [jnp.dot(tri, gt, preferred_element_type=F32) for gt in g_terms]
    kbs = [k * b for k, b in zip(ks, betas)]
    kqs = [_dot_nt(jnp.concatenate([kb, q], axis=0), k) for kb, q, k in zip(kbs, qs, ks)]
    gc_cols, decays = [], []
    for part in parts:
        gc_col = part[:, 0:1] + part[:, 1:2] + part[:, 2:3]
        gc_cb = jnp.broadcast_to(gc_col, (c, c))
        gc_rb = jnp.sum(jnp.where(eye, gc_cb, 0.0), axis=0, keepdims=True)
        decays.append(jnp.where(incl, jnp.exp(jnp.where(incl, gc_cb - gc_rb, 0.0)), 0.0))
        gc_cols.append(gc_col)
    lowers = [jnp.where(strict, kq[0:c] * dec, 0.0) for kq, dec in zip(kqs, decays)]
    a_intras = [jnp.where(incl, kq[c:2 * c] * dec, 0.0) for kq, dec in zip(kqs, decays)]
    tinvs = _unit_lower_inverses(lowers, c)
    egcs = [jnp.exp(gc) for gc in gc_cols]
    uws = [_dot(tinv, jnp.concatenate([v * b, kb * egc], axis=1))
           for tinv, v, b, kb, egc in zip(tinvs, vs, betas, kbs, egcs)]
    s_old = [s_scr[hh] for hh in heads]
    ws_qs = [_dot(jnp.concatenate([uw[:, DK_A:2 * DK_A], q * egc], axis=0), s)
             for uw, q, egc, s in zip(uws, qs, egcs, s_old)]
    v_news = [uw[:, 0:DK_A] - wq[0:c] for uw, wq in zip(uws, ws_qs)]
    outs = [wq[c:2 * c] + _dot(a, vn) for wq, a, vn in zip(ws_qs, a_intras, v_news)]
    for hh in heads:
        g_last = gc_cols[hh][c - 1:c, :]
        kd = ks[hh] * jnp.exp(g_last - gc_cols[hh])
        s_scr[hh] = s_old[hh] * jnp.exp(g_last) + _dot_tn(kd, v_news[hh])
    z_all = z_ref[0]
    for hh in heads:
        o = outs[hh]
        o = o * _rms_scale(o) * gain_ref[...]
        o = o * _silu(z_all[:, hsl(hh)])
        o_ref[0, :, hsl(hh)] = o.astype(o_ref.dtype)

    @pl.when(it == n_tiles - 1)
    def _():
        sout_ref[0] = s_scr[...]


def _gdn(proj, off, conv_w, conv_hist8, s0, a_log, dt_bias, gdn_gain, l, *, tile, hb):
    b, t, _ = proj.shape
    assert t % tile == 0
    tt = tr = tile
    n_tiles = t // tile
    wid = hb * LANE
    cq, ck, cv, cz = (off[n] // wid for n in ("qa", "ka", "va", "za"))
    cba = off["ba"] // LANE
    nhb = H_A // hb

    def tok(col0):
        return pl.BlockSpec((1, tr, wid), lambda bi, hi, ti: (bi, ti, col0 + hi))

    def per_head(col0, rows):
        return _wspec(l, rows, wid, lambda bi, hi, ti: (0, col0 + hi))

    def hist(col0):
        return pl.BlockSpec((1, SUBLANE, wid), lambda bi, hi, ti: (bi, 0, col0 + hi))

    smem = pl.BlockSpec(memory_space=pltpu.SMEM)
    state_spec = pl.BlockSpec((1, hb, DK_A, DK_A), lambda bi, hi, ti: (bi, hi, 0, 0))
    body = functools.partial(_gdn_prompt_body, tt=tt, n_tiles=n_tiles, hb=hb)
    return pl.pallas_call(
        body,
        grid=(b, nhb, n_tiles),
        in_specs=[smem, smem,
                  tok(cq), tok(ck), tok(cv), tok(cz),
                  pl.BlockSpec((1, tr, LANE), lambda bi, hi, ti: (bi, ti, cba)),
                  per_head(0, CONV_K), per_head(nhb, CONV_K), per_head(2 * nhb, CONV_K),
                  hist(0), hist(nhb), hist(2 * nhb),
                  state_spec,
                  _wspec(l, 1, DK_A, lambda bi, hi, ti: (0, 0))],
        out_specs=[pl.BlockSpec((1, tr, wid), lambda bi, hi, ti: (bi, ti, hi)),
                   state_spec],
        out_shape=[jax.ShapeDtypeStruct((b, t, QK_A), BF16),
                   jax.ShapeDtypeStruct((b, H_A, DK_A, DK_A), F32)],
        scratch_shapes=[pltpu.VMEM((hb, DK_A, DK_A), F32),
                        pltpu.VMEM((3, tt + SUBLANE, wid), F32)],
        compiler_params=_cparams(("parallel", "parallel", "arbitrary")),
        name="gdn",
    )(a_log[l], dt_bias[l], proj, proj, proj, proj, proj, conv_w, conv_w, conv_w,
      conv_hist8, conv_hist8, conv_hist8, s0, gdn_gain)


def _gdn_sample_body(alog_ref, dtb_ref, q_ref, k_ref, v_ref, z_ref, ba_ref, cwq_ref, cwk_ref, cwv_ref,
                     hq_ref, hk_ref, hv_ref, s0_ref, gain_ref, o_ref, sout_ref, ext_scr, z_scr, ba_scr, *, tr):
    tp = SUBLANE
    n = H_A * tp
    for i, (x_ref, h_ref) in enumerate(((q_ref, hq_ref), (k_ref, hk_ref), (v_ref, hv_ref))):
        ext_scr[i, 0:SUBLANE, :] = h_ref[0]
        ext_scr[i, SUBLANE:, :] = jnp.zeros((tp, QK_A), F32)
        ext_scr[i, SUBLANE:SUBLANE + tr, :] = x_ref[0]

    def conv_act(i, cw_ref):
        base = SUBLANE - (CONV_K - 1)
        y = ext_scr[i, base:base + tp, :] * cw_ref[0:1, :]
        for j in range(1, CONV_K):
            y = y + ext_scr[i, base + j:base + j + tp, :] * cw_ref[j:j + 1, :]
        return _silu(y)

    def pack(x):
        return jnp.concatenate([x[:, h * LANE:(h + 1) * LANE] for h in range(H_A)], axis=0)

    q = pack(conv_act(0, cwq_ref))
    k = pack(conv_act(1, cwk_ref))
    v = pack(conv_act(2, cwv_ref))
    q = q * lax.rsqrt(jnp.sum(q * q, axis=-1, keepdims=True) + L2_EPS) * (DK_A ** -0.5)
    k = k * lax.rsqrt(jnp.sum(k * k, axis=-1, keepdims=True) + L2_EPS)

    ba_scr[...] = jnp.zeros((tp, LANE), F32)
    ba_scr[0:tr, :] = ba_ref[0]
    ba = ba_scr[...]
    b_raw = jnp.concatenate([ba[:, h:h + 1] for h in range(H_A)], axis=0)
    a_raw = jnp.concatenate([ba[:, H_A + h:H_A + h + 1] for h in range(H_A)], axis=0)
    a_log = jnp.concatenate([jnp.full((tp, 1), alog_ref[h], F32) for h in range(H_A)], axis=0)
    dtb = jnp.concatenate([jnp.full((tp, 1), dtb_ref[h], F32) for h in range(H_A)], axis=0)
    beta = jax.nn.sigmoid(b_raw)
    xs = a_raw + dtb
    g = -jnp.exp(a_log) * (jnp.maximum(xs, 0.0) + jnp.log1p(jnp.exp(-jnp.abs(xs))))

    valid = (lax.broadcasted_iota(jnp.int32, (n, 1), 0) & (tp - 1)) < tr
    q = jnp.where(valid, q, 0.0)
    k = jnp.where(valid, k, 0.0)
    v = jnp.where(valid, v, 0.0)
    beta = jnp.where(valid, beta, 0.0)
    g = jnp.where(valid, g, 0.0)

    row = lax.broadcasted_iota(jnp.int32, (n, n), 0)
    col = lax.broadcasted_iota(jnp.int32, (n, n), 1)
    same_head = (row >> 3) == (col >> 3)
    incl = jnp.logical_and(same_head, row >= col)
    strict = jnp.logical_and(same_head, row > col)
    eye = row == col
    sum_mat = jnp.concatenate([jnp.where(incl, 1.0, 0.0), jnp.where(same_head, 1.0, 0.0)], axis=0).astype(BF16)
    g1, g2, g3 = _split3(jnp.broadcast_to(g, (n, LANE)))
    lane_c = lax.broadcasted_iota(jnp.int32, (n, LANE), 1)
    g_terms = jnp.where(lane_c == 0, g1, jnp.where(lane_c == 1, g2, g3))
    part = jnp.dot(sum_mat, g_terms.astype(BF16), preferred_element_type=F32)
    sums = part[:, 0:1] + part[:, 1:2] + part[:, 2:3]
    gc_col = sums[0:n]
    g_last = sums[n:2 * n]
    gc_cb = jnp.broadcast_to(gc_col, (n, n))
    gc_rb = jnp.sum(jnp.where(eye, gc_cb, 0.0), axis=0, keepdims=True)
    decay = jnp.where(incl, jnp.exp(jnp.where(incl, gc_cb - gc_rb, 0.0)), 0.0)
    kb = k * beta
    kq = _dot_nt(jnp.concatenate([kb, q], axis=0), k)
    lower = jnp.where(strict, kq[0:n] * decay, 0.0)
    a_intra = jnp.where(incl, kq[n:2 * n] * decay, 0.0)
    tinv = _unit_lower_inverse(lower, tp)
    egc = jnp.exp(gc_col)
    uw = _dot(tinv, jnp.concatenate([v * beta, kb * egc], axis=1))
    kd = k * jnp.exp(g_last - gc_col)

    row_e = lax.broadcasted_iota(jnp.int32, (n, QK_A), 0)
    col_e = lax.broadcasted_iota(jnp.int32, (n, QK_A), 1)
    own = (row_e >> 3) == (col_e >> 7)

    def expand(x):
        return jnp.where(own, jnp.concatenate([x] * H_A, axis=1), 0.0)

    s = s0_ref[0].reshape(H_A * DK_A, DK_A)
    ws_qs = _dot(jnp.concatenate([expand(uw[:, DK_A:2 * DK_A]), expand(q * egc)], axis=0), s)
    v_new = uw[:, 0:DK_A] - ws_qs[0:n]
    o = ws_qs[n:2 * n] + _dot(a_intra, v_new)
    upd = _dot_tn(expand(kd), v_new)
    for h in range(H_A):
        eg = jnp.exp(g_last[h * tp:h * tp + 1, :])
        sout_ref[0, h] = s[h * DK_A:(h + 1) * DK_A] * eg + upd[h * DK_A:(h + 1) * DK_A]

    o = o * _rms_scale(o) * gain_ref[...]
    z_scr[...] = jnp.zeros((tp, QK_A), F32)
    z_scr[0:tr, :] = z_ref[0]
    o = o * _silu(pack(z_scr[...]))
    o_tok = jnp.concatenate([o[h * tp:(h + 1) * tp] for h in range(H_A)], axis=1)
    o_ref[0] = o_tok[0:tr].astype(o_ref.dtype)


def _gdn_sample(proj, off, conv_w, conv_hist8, s0, a_log, dt_bias, gdn_gain, l):
    b, t, _ = proj.shape
    assert t <= SUBLANE
    cq, ck, cv, cz = (off[n] // QK_A for n in ("qa", "ka", "va", "za"))
    cba = off["ba"] // LANE

    def tok(col0):
        return pl.BlockSpec((1, t, QK_A), lambda bi: (bi, 0, col0))

    def conv(col0):
        return _wspec(l, CONV_K, QK_A, lambda bi: (0, col0))

    def hist(col0):
        return pl.BlockSpec((1, SUBLANE, QK_A), lambda bi: (bi, 0, col0))

    smem = pl.BlockSpec(memory_space=pltpu.SMEM)
    state_spec = pl.BlockSpec((1, H_A, DK_A, DK_A), lambda bi: (bi, 0, 0, 0))
    return pl.pallas_call(
        functools.partial(_gdn_sample_body, tr=t),
        grid=(b,),
        in_specs=[smem, smem, tok(cq), tok(ck), tok(cv), tok(cz),
                  pl.BlockSpec((1, t, LANE), lambda bi: (bi, 0, cba)),
                  conv(0), conv(1), conv(2), hist(0), hist(1), hist(2),
                  state_spec, _wspec(l, 1, DK_A, lambda bi: (0, 0))],
        out_specs=[pl.BlockSpec((1, t, QK_A), lambda bi: (bi, 0, 0)), state_spec],
        out_shape=[jax.ShapeDtypeStruct((b, t, QK_A), BF16),
                   jax.ShapeDtypeStruct((b, H_A, DK_A, DK_A), F32)],
        scratch_shapes=[pltpu.VMEM((3, 2 * SUBLANE, QK_A), F32),
                        pltpu.VMEM((SUBLANE, QK_A), F32),
                        pltpu.VMEM((SUBLANE, LANE), F32)],
        compiler_params=_cparams(("parallel",)),
        name="gdn_sample",
    )(a_log[l], dt_bias[l], proj, proj, proj, proj, proj, conv_w, conv_w, conv_w,
      conv_hist8, conv_hist8, conv_hist8, s0, gdn_gain)


def _attn_body(q0_ref, q1_ref, q2_ref, k0_ref, k1_ref, k2_ref, v0_ref, v1_ref, v2_ref, o_ref,
               kb_scr, vb_scr, d0_scr, d1_scr, d2_scr, *, s_len):
    q_refs = (q0_ref, q1_ref, q2_ref)
    k_refs = (k0_ref, k1_ref, k2_ref)
    v_refs = (v0_ref, v1_ref, v2_ref)
    dist_scrs = (d0_scr, d1_scr, d2_scr)
    tq_n = Q_TILE
    scale = HD_B ** -0.5
    for gi, (win, dil) in enumerate(DIL_GROUPS):
        kb_scr[gi] = k_refs[gi][0].astype(BF16)
        vb_scr[gi] = v_refs[gi][0].astype(BF16)
        nk = dist_scrs[gi].shape[1]
        rc = lax.broadcasted_iota(jnp.int32, (tq_n, nk), 0) - lax.broadcasted_iota(jnp.int32, (tq_n, nk), 1)
        dist_scrs[gi][...] = jnp.where((rc & (dil - 1)) == 0, rc, WRONG_RESIDUE).astype(F32)

    def tile(i, carry, cap):
        q0 = pl.multiple_of(i * tq_n, tq_n)
        scores = []
        starts = []
        m = None
        for gi, (win, dil) in enumerate(DIL_GROUPS):
            nk = min(cap, win + tq_n)
            k0 = pl.multiple_of(jnp.clip(q0 - win, 0, cap - nk), tq_n)
            q = q_refs[gi][0, pl.ds(q0, tq_n), :]
            s = _dot_nt(q, kb_scr[gi, pl.ds(k0, nk), :]) * scale
            d = dist_scrs[gi][:, 0:nk] + (q0 - k0).astype(F32)
            ok = d >= 0.0
            if win < s_len - 1:
                ok = jnp.logical_and(ok, d <= float(win))
            s = jnp.where(ok, s, -jnp.inf)
            mg = jnp.max(s, axis=1, keepdims=True)
            m = mg if m is None else jnp.maximum(m, mg)
            scores.append(s)
            starts.append((k0, nk))
        acc = jnp.zeros((tq_n, HD_B), F32)
        den = jnp.zeros((tq_n, 1), F32)
        for gi in range(N_GROUPS):
            k0, nk = starts[gi]
            p = jnp.exp(scores[gi] - m)
            den = den + jnp.sum(p, axis=1, keepdims=True)
            acc = acc + _dot(p, vb_scr[gi, pl.ds(k0, nk), :])
        o_ref[0, pl.ds(q0, tq_n), :] = (acc / den).astype(o_ref.dtype)
        return carry

    n_tiles = s_len // tq_n
    seg = max(1, n_tiles // ATTN_SEGMENTS)
    for lo in range(0, n_tiles, seg):
        hi = min(lo + seg, n_tiles)
        lax.fori_loop(lo, hi, functools.partial(tile, cap=hi * tq_n), 0)


def _attn_prompt(proj, off):
    b, s_len, _ = proj.shape
    cq, ck, cv = (off[n] // LANE for n in ("qb", "kb", "vb"))

    def spec(col0, gi):
        return pl.BlockSpec((1, s_len, LANE), lambda bi, hi: (bi, 0, col0 + gi * H_G + hi))

    return pl.pallas_call(
        functools.partial(_attn_body, s_len=s_len),
        grid=(b, H_G),
        in_specs=[spec(c0, gi) for c0 in (cq, ck, cv) for gi in range(N_GROUPS)],
        out_specs=pl.BlockSpec((1, s_len, LANE), lambda bi, hi: (bi, 0, hi)),
        out_shape=jax.ShapeDtypeStruct((b, s_len, OUT_B), BF16),
        scratch_shapes=[pltpu.VMEM((N_GROUPS, s_len, HD_B), BF16),
                        pltpu.VMEM((N_GROUPS, s_len, HD_B), BF16)]
        + [pltpu.VMEM((Q_TILE, min(s_len, win + Q_TILE)), F32) for (win, _) in DIL_GROUPS],
        compiler_params=_cparams(("parallel", "parallel")),
        name="attn_prompt",
    )(*([proj] * 9))


SHIFT_UNITS = 128


def _shift_body(a_ref, b_ref, o_ref):
    r = a_ref.shape[0]
    o_ref[0:r - 1] = a_ref[1:r]
    o_ref[r - 1:r] = b_ref[...]


def _shift_cache(c, t_new):
    depth, batch, win = c.shape[:3]
    nu = win // t_new
    r = min(SHIFT_UNITS, nu)
    assert win % t_new == 0 and nu % r == 0
    view = c.reshape(depth, batch, nu, t_new, 2, H_G, HD_B)
    tail = (t_new, 2, H_G, HD_B)
    zeros = (0,) * len(tail)
    out = pl.pallas_call(
        _shift_body,
        grid=(depth, batch, nu // r),
        in_specs=[pl.BlockSpec((None, None, r) + tail, lambda l, b, j: (l, b, j) + zeros),
                  pl.BlockSpec((None, None, 1) + tail,
                               lambda l, b, j: (l, b, jnp.minimum((j + 1) * r, nu - 1)) + zeros)],
        out_specs=pl.BlockSpec((None, None, r) + tail, lambda l, b, j: (l, b, j) + zeros),
        out_shape=jax.ShapeDtypeStruct(view.shape, view.dtype),
        compiler_params=_cparams(("parallel", "parallel", "parallel")),
        name="cache_shift",
    )(view, view)
    return out.reshape(c.shape)


def _attn_sample_body(q_ref, kn_ref, vn_ref, c1_ref, c2_ref, c3_ref, w1_in, w2_in, w3_in,
                      o_ref, w1_ref, w2_ref, w3_ref, *, t_new):
    del w1_in, w2_in, w3_in
    scale = HD_B ** -0.5
    q = q_ref[0]
    kn = kn_ref[0]
    vn = vn_ref[0]
    caches = (c1_ref, c2_ref, c3_ref)
    wins = (w1_ref, w2_ref, w3_ref)
    row_c = lax.broadcasted_iota(jnp.int32, (N_BACK, H_G, 1), 0)

    def heads(x, t, gi):
        c0 = gi * OUT_B
        return jnp.concatenate([x[t:t + 1, c0 + h * HD_B:c0 + (h + 1) * HD_B] for h in range(H_G)], axis=0)

    k_new = [[heads(kn, t, gi) for t in range(t_new)] for gi in range(N_GROUPS)]
    v_new = [[heads(vn, t, gi) for t in range(t_new)] for gi in range(N_GROUPS)]
    for gi in range(N_GROUPS):
        for t in range(t_new):
            wins[gi][0, t, 0] = k_new[gi][t]
            wins[gi][0, t, 1] = v_new[gi][t]

    for t in range(t_new):
        parts = []
        for gi in range(N_GROUPS):
            qv = heads(q, t, gi)
            r = 0 if gi == 0 else t
            kc = caches[gi][0, :, r, 0]
            vc = caches[gi][0, :, r, 1]
            s_c = jnp.sum(kc * qv[None], axis=-1, keepdims=True) * scale
            if gi == 0:
                s_c = jnp.where(row_c >= t, s_c, -jnp.inf)
                news = list(range(t + 1))
            else:
                news = [t]
            s_n = [jnp.sum(k_new[gi][tn] * qv, axis=-1, keepdims=True) * scale for tn in news]
            m = jnp.max(s_c, axis=0)
            for sn in s_n:
                m = jnp.maximum(m, sn)
            p_c = jnp.exp(s_c - m[None])
            l = jnp.sum(p_c, axis=0)
            o = jnp.sum(p_c * vc, axis=0)
            for tn, sn in zip(news, s_n):
                p_n = jnp.exp(sn - m)
                l = l + p_n
                o = o + p_n * v_new[gi][tn]
            parts.append((o, m, l))
        m_all = jnp.maximum(jnp.maximum(parts[0][1], parts[1][1]), parts[2][1])
        num = 0.0
        den = 0.0
        for (o, m, l) in parts:
            e = jnp.exp(m - m_all)
            num = num + e * o
            den = den + e * l
        out = (num / den).astype(o_ref.dtype)
        for h in range(H_G):
            o_ref[0, t:t + 1, h * HD_B:(h + 1) * HD_B] = out[h:h + 1, :]


def _attn_sample(proj, off, caches, windows, l):
    b, t, _ = proj.shape
    depth = caches[0].shape[0]
    assert t <= DIL_GROUPS[1][1] and DIL_GROUPS[0][1] == 1
    views, cspecs = [], []
    for c, (win, dil) in zip(caches, DIL_GROUPS):
        views.append(c.reshape(depth, b, N_BACK, dil, 2, H_G, HD_B))
        rows = min(dil, t)
        cspecs.append(pl.BlockSpec((None, 1, N_BACK, rows, 2, H_G, HD_B),
                                   lambda bi: (l, bi, 0, 0, 0, 0, 0)))
    wspecs = [pl.BlockSpec((None, 1, t, 2, H_G, HD_B),
                           functools.partial(lambda bi, blk: (l, bi, blk, 0, 0, 0), blk=win // t - 1))
              for (win, _) in DIL_GROUPS]

    def pspec(name):
        return pl.BlockSpec((1, t, QKV_B), lambda bi: (bi, 0, off[name] // QKV_B))

    any_spec = pl.BlockSpec(memory_space=pl.ANY)
    res = pl.pallas_call(
        functools.partial(_attn_sample_body, t_new=t),
        grid=(b,),
        in_specs=[pspec("qb"), pspec("kb"), pspec("vb")] + cspecs + [any_spec] * 3,
        out_specs=[pl.BlockSpec((1, t, OUT_B), lambda bi: (bi, 0, 0))] + wspecs,
        out_shape=[jax.ShapeDtypeStruct((b, t, OUT_B), BF16)]
        + [jax.ShapeDtypeStruct(w.shape, w.dtype) for w in windows],
        input_output_aliases={6: 1, 7: 2, 8: 3},
        compiler_params=_cparams(("parallel",)),
        name="attn_sample",
    )(proj, proj, proj, *views, *windows)
    return res[0], tuple(res[1:])


def _pool_body(u_ref, hist_ref, wp_ref, ps_ref, o_ref, ext_scr, *, tt, tr, pos0, n_tiles):
    hp = POOL_HIST + 1
    it = pl.program_id(1)

    @pl.when(it == 0)
    def _():
        ext_scr[0:hp, :] = hist_ref[0]

    if tr < tt:
        ext_scr[hp:, :] = jnp.zeros((tt, C_POOL), F32)
    ext_scr[hp:hp + tr, :] = u_ref[0]

    pos = pos0 + it * tt + lax.broadcasted_iota(jnp.int32, (tt, 1), 0)
    outs = []
    for gi, win in enumerate(POOL_WINDOWS):
        lo, hi = gi * CG, (gi + 1) * CG
        x = ext_scr[hp:hp + tt, lo:hi]
        acc = x
        for back in range(1, win):
            acc = acc + ext_scr[hp - back:hp - back + tt, lo:hi]
        cnt = jnp.minimum(win, pos + 1).astype(F32)
        pooled = acc / cnt - x
        outs.append(jnp.dot(pooled.astype(BF16), wp_ref[gi], preferred_element_type=F32))
    oc = jnp.concatenate(outs, axis=1) * ps_ref[...]
    o_ref[0] = oc[0:tr].astype(o_ref.dtype)

    if n_tiles > 1:
        ext_scr[0:hp, :] = ext_scr[tt:tt + hp, :]


def _pool(proj, off, hist16, w_pool, pool_scale, l, pos0, tile):
    b, t, _ = proj.shape
    hp = POOL_HIST + 1
    if t >= tile:
        tt = tr = tile
        n_tiles = t // tile
    else:
        tt, tr, n_tiles = SUBLANE, t, 1
    cu = off["uc"] // C_POOL
    assert off["uc"] % C_POOL == 0
    ng = len(POOL_WINDOWS)
    return pl.pallas_call(
        functools.partial(_pool_body, tt=tt, tr=tr, pos0=pos0, n_tiles=n_tiles),
        grid=(b, n_tiles),
        in_specs=[pl.BlockSpec((1, tr, C_POOL), lambda bi, ti: (bi, ti, cu)),
                  pl.BlockSpec((1, hp, C_POOL), lambda bi, ti: (bi, 0, 0)),
                  pl.BlockSpec((None, ng, CG, CG), lambda bi, ti: (l, 0, 0, 0)),
                  _wspec(l, 1, C_POOL, lambda bi, ti: (0, 0))],
        out_specs=pl.BlockSpec((1, tr, C_POOL), lambda bi, ti: (bi, ti, 0)),
        out_shape=jax.ShapeDtypeStruct((b, t, C_POOL), BF16),
        scratch_shapes=[pltpu.VMEM((hp + tt, C_POOL), F32)],
        compiler_params=_cparams(("parallel", "arbitrary")),
        name="pool",
    )(proj, hist16, w_pool, pool_scale)


def _tile_rows(m, pref):
    return pref if m % pref == 0 else m


def _layer(x, wts, l, conv_hist, s0, pool_hist, caches, windows, prompt):
    (w_in, conv_w, a_log, dt_bias, gdn_gain, w_pool, pool_scale, w_a, w_b, w_c, w_out, w_gu, w_down,
     g_pre_mix, g_post_mix, g_pre_ffn, g_post_ffn) = wts
    b, t, d = x.shape
    m = b * t
    off, npad = _proj_layout(d)
    x2 = x.reshape(m, d)
    tm = _tile_rows(m, TM_NORM)
    tm_wide = _tile_rows(m, TM_WIDE)

    proj2 = _norm_matmul(x2, g_pre_mix, w_in, l, tm_wide, TN)
    proj = proj2.reshape(b, t, npad)

    hist8 = jnp.concatenate([jnp.zeros((b, SUBLANE - (CONV_K - 1), 3 * QK_A), F32), conv_hist], axis=1)
    if prompt:
        out_a, s_new = _gdn(proj, off, conv_w, hist8, s0, a_log, dt_bias, gdn_gain, l, tile=GDN_TILE, hb=4)
    else:
        out_a, s_new = _gdn_sample(proj, off, conv_w, hist8, s0, a_log, dt_bias, gdn_gain, l)
    qa0 = off["qa"]
    conv_new = jnp.concatenate([conv_hist, proj[:, :, qa0:qa0 + 3 * QK_A]], axis=1)[:, -(CONV_K - 1):]

    kb0, vb0 = off["kb"], off["vb"]
    if prompt:
        out_b = _attn_prompt(proj, off).reshape(m, OUT_B)
        win_new = []
        for gi, (win, _) in enumerate(DIL_GROUPS):
            rows = min(win, t)
            kg = proj[:, t - rows:, kb0 + gi * OUT_B:kb0 + (gi + 1) * OUT_B].reshape(b, rows, 1, H_G, HD_B)
            vg = proj[:, t - rows:, vb0 + gi * OUT_B:vb0 + (gi + 1) * OUT_B].reshape(b, rows, 1, H_G, HD_B)
            win_new.append(jnp.concatenate([kg, vg], axis=2))
    else:
        out_b, win_new = _attn_sample(proj, off, caches, windows, l)
        out_b = out_b.reshape(m, OUT_B)

    hist16 = jnp.concatenate([jnp.zeros((b, 1, C_POOL), F32), pool_hist], axis=1)
    out_c = _pool(proj, off, hist16, w_pool, pool_scale, l, 0 if prompt else SAMPLE_POS0, POOL_TILE)
    uc0 = off["uc"]
    pool_new = jnp.concatenate([pool_hist, proj[:, :, uc0:uc0 + C_POOL]], axis=1)[:, -POOL_HIST:]

    merged = _merge(out_a.reshape(m, QK_A), out_b, out_c.reshape(m, C_POOL), w_a, w_b, w_c, proj2, off, l, tm_wide, TN)
    x2 = _mm_norm_res(merged, w_out, x2, g_post_mix, l, tm, d)
    act = _ffn_up(x2, g_pre_ffn, w_gu, l, tm_wide, TN)
    d_ff = act.shape[1]
    x2 = _mm_norm_res(act, w_down, x2, g_post_ffn, l, tm, d_ff // 4 if d_ff % (4 * LANE) == 0 else 512)
    return x2.reshape(b, t, d), (win_new[0], win_new[1], win_new[2], s_new, conv_new, pool_new)


def kernel(x_prompt, x_sample, cache_win1, cache_win2, cache_win3, state_gdn, state_conv, state_pool,
           w_in, conv_w, a_log, dt_bias, gdn_gain, w_pool, pool_scale, w_br_a, w_br_b, w_br_c,
           w_out, w_gu, w_down, g_pre_mix, g_post_mix, g_pre_ffn, g_post_ffn):
    depth = w_in.shape[0]
    bp = x_prompt.shape[0]
    d = x_prompt.shape[-1]

    def row(v):
        return v.reshape(depth, 1, v.shape[-1])

    wts = (jnp.swapaxes(w_in, 1, 2).astype(BF16), conv_w, a_log, dt_bias, row(gdn_gain),
           w_pool.astype(BF16), row(pool_scale),
           w_br_a.astype(BF16), w_br_b.astype(BF16), w_br_c.astype(BF16),
           w_out.astype(BF16), w_gu.astype(BF16), w_down.astype(BF16),
           row(g_pre_mix), row(g_post_mix), row(g_pre_ffn), row(g_post_ffn))
    yp, ys = x_prompt, x_sample
    caches = (cache_win1, cache_win2, cache_win3)
    windows = tuple(_shift_cache(c, x_sample.shape[1]) for c in caches)
    new_p, new_s = [], []
    for l in range(depth):
        yp, st_p = _layer(yp, wts, l,
                          jnp.zeros((bp, CONV_K - 1, 3 * QK_A), F32),
                          jnp.zeros((bp, H_A, DK_A, DK_A), F32),
                          jnp.zeros((bp, POOL_HIST, C_POOL), F32),
                          None, None, True)
        ys, st_s = _layer(ys, wts, l, state_conv[l], state_gdn[l], state_pool[l],
                          caches, windows, False)
        windows = st_s[:3]
        new_p.append(st_p)
        new_s.append(st_s)
    outs_p = [jnp.stack([st[i] for st in new_p], axis=0) for i in range(6)]
    outs_s = [jnp.stack([st[i] for st in new_s], axis=0) for i in range(3, 6)]
    return (yp, ys, *outs_p, *windows, *outs_s)
```

```python
import functools

import jax
import jax.numpy as jnp
from jax import lax
from jax.experimental import pallas as pl
from jax.experimental.pallas import tpu as pltpu

F32 = jnp.float32
BF16 = jnp.bfloat16

H_A = 8
DK_A = 128
CONV_K = 4
DIL_GROUPS = ((128, 1), (512, 4), (2048, 16))
N_GROUPS = len(DIL_GROUPS)
H_G = 4
HD_B = 128
N_BACK = 128
POOL_WINDOWS = (2, 4, 8, 16)
CG = 256
C_POOL = CG * len(POOL_WINDOWS)
POOL_HIST = max(POOL_WINDOWS) - 1
SAMPLE_POS0 = 8192
EPS = 1e-6
L2_EPS = 1e-6

LANE = 128
SUBLANE = 8
VMEM_LIMIT = 48 * 1024 * 1024

QK_A = H_A * DK_A
QKV_B = N_GROUPS * H_G * HD_B
OUT_B = H_G * HD_B
TM_WIDE = 1024
TM_NORM = 512
TN = 512
POOL_TILE = 256
GDN_TILE = 256
Q_TILE = 128
ATTN_SEGMENTS = 4
WRONG_RESIDUE = -(1 << 24)


def _proj_layout(d_model):
    off = {}
    c = 0
    for name, width in (("qb", QKV_B), ("kb", QKV_B), ("vb", QKV_B),
                        ("ba", H_A), ("aa", H_A), ("pad0", 512 - 2 * H_A),
                        ("uc", C_POOL),
                        ("qa", QK_A), ("ka", QK_A), ("va", QK_A), ("za", QK_A),
                        ("ga", d_model), ("gb", d_model), ("gc", d_model)):
        off[name] = c
        c += width
    total = -(-c // 512) * 512
    return off, total


def _cparams(sem):
    return pltpu.CompilerParams(dimension_semantics=sem, vmem_limit_bytes=VMEM_LIMIT)


def _dot(a, b):
    return jnp.dot(a.astype(BF16), b.astype(BF16), preferred_element_type=F32)


def _dot_nt(a, b):
    return lax.dot_general(a.astype(BF16), b.astype(BF16), (((1,), (1,)), ((), ())),
                           preferred_element_type=F32)


def _dot_tn(a, b):
    return lax.dot_general(a.astype(BF16), b.astype(BF16), (((0,), (0,)), ((), ())),
                           preferred_element_type=F32)


def _split3(x):
    x1 = x.astype(BF16).astype(F32)
    r1 = x - x1
    x2 = r1.astype(BF16).astype(F32)
    x3 = (r1 - x2).astype(BF16).astype(F32)
    return x1, x2, x3


def _rms_scale(y):
    return lax.rsqrt(jnp.mean(y * y, axis=-1, keepdims=True) + EPS)


def _silu(x):
    return x * jax.nn.sigmoid(x)


def _wspec(l, rows, cols, index):
    return pl.BlockSpec((None, rows, cols), lambda *g: (l,) + index(*g))


def _in_proj_tile_rows(d_model, tn):
    src = {}
    c = 0
    for name, width in (("qa", QK_A), ("ka", QK_A), ("va", QK_A), ("za", QK_A), ("ba", H_A), ("aa", H_A),
                        ("qb", QKV_B), ("kb", QKV_B), ("vb", QKV_B), ("uc", C_POOL),
                        ("ga", d_model), ("gb", d_model), ("gc", d_model)):
        src[name] = c
        c += width
    off, total = _proj_layout(d_model)
    names = [n for n in sorted(off, key=off.get) if n in src]
    rows = []
    for j in range(total // tn):
        owner = max((n for n in names if off[n] <= j * tn), key=off.get)
        row0 = src[owner] + j * tn - off[owner]
        assert row0 % 16 == 0 and row0 + tn <= c
        for n in names:
            if j * tn <= off[n] < (j + 1) * tn:
                assert src[n] - row0 == off[n] - j * tn
        rows.append(row0)
    return rows


def _in_proj_body(rows_ref, x_ref, g_ref, w_hbm, o_ref, h_ref, wbuf, sem, *, l, tn, nj, n_steps):
    j = pl.program_id(1)
    step = pl.program_id(0) * nj + j
    slot = lax.rem(step, 2)

    def fetch(tile, slot_):
        row0 = pl.multiple_of(rows_ref[tile], 16)
        return pltpu.make_async_copy(w_hbm.at[l, pl.ds(row0, tn), :], wbuf.at[slot_], sem.at[slot_])

    @pl.when(step == 0)
    def _():
        fetch(0, 0).start()

    @pl.when(step + 1 < n_steps)
    def _():
        fetch(jnp.where(j + 1 == nj, 0, j + 1), 1 - slot).start()

    @pl.when(j == 0)
    def _():
        xf = x_ref[...]
        h_ref[...] = (xf * _rms_scale(xf) * g_ref[...]).astype(BF16)

    fetch(j, slot).wait()
    o_ref[...] = lax.dot_general(h_ref[...], wbuf[slot], (((1,), (1,)), ((), ())), preferred_element_type=F32)


def _norm_matmul(x, gain, w_t, l, tm, tn):
    m, d = x.shape
    _, n = _proj_layout(d)
    rows = jnp.asarray(_in_proj_tile_rows(d, tn), jnp.int32)
    ni, nj = m // tm, n // tn
    return pl.pallas_call(
        functools.partial(_in_proj_body, l=l, tn=tn, nj=nj, n_steps=ni * nj),
        grid=(ni, nj),
        in_specs=[pl.BlockSpec(memory_space=pltpu.SMEM),
                  pl.BlockSpec((tm, d), lambda i, j: (i, 0)),
                  _wspec(l, 1, d, lambda i, j: (0, 0)),
                  pl.BlockSpec(memory_space=pl.ANY)],
        out_specs=pl.BlockSpec((tm, tn), lambda i, j: (i, j)),
        out_shape=jax.ShapeDtypeStruct((m, n), F32),
        scratch_shapes=[pltpu.VMEM((tm, d), BF16),
                        pltpu.VMEM((2, tn, d), BF16),
                        pltpu.SemaphoreType.DMA((2,))],
        compiler_params=_cparams(("arbitrary", "arbitrary")),
        name="in_proj",
    )(rows, x, gain, w_t)


def _ffn_up_body(x_ref, g_ref, wg_ref, wu_ref, o_ref, h_ref):
    @pl.when(pl.program_id(1) == 0)
    def _():
        xf = x_ref[...]
        h_ref[...] = (xf * _rms_scale(xf) * g_ref[...]).astype(BF16)

    h = h_ref[...]
    gate = jnp.dot(h, wg_ref[...], preferred_element_type=F32)
    up = jnp.dot(h, wu_ref[...], preferred_element_type=F32)
    o_ref[...] = (_silu(gate) * up).astype(BF16)


def _ffn_up(x, gain, w_gu, l, tm, tn):
    m, d = x.shape
    d_ff = w_gu.shape[-1] // 2
    nj = d_ff // tn
    return pl.pallas_call(
        _ffn_up_body,
        grid=(m // tm, nj),
        in_specs=[pl.BlockSpec((tm, d), lambda i, j: (i, 0)),
                  _wspec(l, 1, d, lambda i, j: (0, 0)),
                  _wspec(l, d, tn, lambda i, j: (0, j)),
                  _wspec(l, d, tn, lambda i, j: (0, j + nj))],
        out_specs=pl.BlockSpec((tm, tn), lambda i, j: (i, j)),
        out_shape=jax.ShapeDtypeStruct((m, d_ff), BF16),
        scratch_shapes=[pltpu.VMEM((tm, d), BF16)],
        compiler_params=_cparams(("parallel", "arbitrary")),
        name="ffn_up",
    )(x, gain, w_gu, w_gu)


def _mm_norm_res_body(a_ref, w_ref, res_ref, g_ref, o_ref, acc_ref, *, nk):
    k = pl.program_id(1)

    @pl.when(k == 0)
    def _():
        acc_ref[...] = jnp.zeros_like(acc_ref)

    acc_ref[...] += jnp.dot(a_ref[...], w_ref[...], preferred_element_type=F32)

    @pl.when(k == nk - 1)
    def _():
        y = acc_ref[...]
        o_ref[...] = res_ref[...] + y * _rms_scale(y) * g_ref[...]


def _mm_norm_res_single_body(a_ref, w_ref, res_ref, g_ref, o_ref):
    y = jnp.dot(a_ref[...], w_ref[...], preferred_element_type=F32)
    o_ref[...] = res_ref[...] + y * _rms_scale(y) * g_ref[...]


def _mm_norm_res(a, w, res, gain, l, tm, tk):
    m, kdim = a.shape
    d = w.shape[-1]
    nk = kdim // tk
    assert nk * tk == kdim
    if nk == 1:
        return pl.pallas_call(
            _mm_norm_res_single_body,
            grid=(m // tm,),
            in_specs=[pl.BlockSpec((tm, kdim), lambda i: (i, 0)),
                      _wspec(l, kdim, d, lambda i: (0, 0)),
                      pl.BlockSpec((tm, d), lambda i: (i, 0)),
                      _wspec(l, 1, d, lambda i: (0, 0))],
            out_specs=pl.BlockSpec((tm, d), lambda i: (i, 0)),
            out_shape=jax.ShapeDtypeStruct((m, d), F32),
            compiler_params=_cparams(("parallel",)),
            name="mm_norm_res",
        )(a, w, res, gain)
    return pl.pallas_call(
        functools.partial(_mm_norm_res_body, nk=nk),
        grid=(m // tm, nk),
        in_specs=[pl.BlockSpec((tm, tk), lambda i, k: (i, k)),
                  _wspec(l, tk, d, lambda i, k: (k, 0)),
                  pl.BlockSpec((tm, d), lambda i, k: (i, 0)),
                  _wspec(l, 1, d, lambda i, k: (0, 0))],
        out_specs=pl.BlockSpec((tm, d), lambda i, k: (i, 0)),
        out_shape=jax.ShapeDtypeStruct((m, d), F32),
        scratch_shapes=[pltpu.VMEM((tm, d), F32)],
        compiler_params=_cparams(("parallel", "arbitrary")),
        name="mm_norm_res",
    )(a, w, res, gain)


def _merge_body(a_ref, b_ref, c_ref, wa_ref, wb_ref, wc_ref, ga_ref, gb_ref, gc_ref, o_ref):
    m = jax.nn.sigmoid(ga_ref[...]) * jnp.dot(a_ref[...], wa_ref[...], preferred_element_type=F32)
    m = m + jax.nn.sigmoid(gb_ref[...]) * jnp.dot(b_ref[...], wb_ref[...], preferred_element_type=F32)
    m = m + jax.nn.sigmoid(gc_ref[...]) * jnp.dot(c_ref[...], wc_ref[...], preferred_element_type=F32)
    o_ref[...] = m.astype(BF16)


def _merge(out_a, out_b, out_c, w_a, w_b, w_c, proj2d, off, l, tm, tn):
    m = out_a.shape[0]
    d = w_a.shape[-1]
    ja, jb, jc = off["ga"] // tn, off["gb"] // tn, off["gc"] // tn
    return pl.pallas_call(
        _merge_body,
        grid=(m // tm, d // tn),
        in_specs=[pl.BlockSpec((tm, out_a.shape[1]), lambda i, j: (i, 0)),
                  pl.BlockSpec((tm, out_b.shape[1]), lambda i, j: (i, 0)),
                  pl.BlockSpec((tm, out_c.shape[1]), lambda i, j: (i, 0)),
                  _wspec(l, w_a.shape[1], tn, lambda i, j: (0, j)),
                  _wspec(l, w_b.shape[1], tn, lambda i, j: (0, j)),
                  _wspec(l, w_c.shape[1], tn, lambda i, j: (0, j)),
                  pl.BlockSpec((tm, tn), lambda i, j: (i, j + ja)),
                  pl.BlockSpec((tm, tn), lambda i, j: (i, j + jb)),
                  pl.BlockSpec((tm, tn), lambda i, j: (i, j + jc))],
        out_specs=pl.BlockSpec((tm, tn), lambda i, j: (i, j)),
        out_shape=jax.ShapeDtypeStruct((m, d), BF16),
        compiler_params=_cparams(("parallel", "arbitrary")),
        name="merge",
    )(out_a, out_b, out_c, w_a, w_b, w_c, proj2d, proj2d, proj2d)


def _merge_out_body(a_ref, b_ref, c_ref, wa_ref, wb_ref, wc_ref, ga_ref, gb_ref, gc_ref, wo_ref, res_ref, g_ref,
                    o_ref, acc_ref, *, nk):
    k = pl.program_id(1)

    @pl.when(k == 0)
    def _():
        acc_ref[...] = jnp.zeros_like(acc_ref)

    m = jax.nn.sigmoid(ga_ref[...]) * jnp.dot(a_ref[...], wa_ref[...], preferred_element_type=F32)
    m = m + jax.nn.sigmoid(gb_ref[...]) * jnp.dot(b_ref[...], wb_ref[...], preferred_element_type=F32)
    m = m + jax.nn.sigmoid(gc_ref[...]) * jnp.dot(c_ref[...], wc_ref[...], preferred_element_type=F32)
    acc_ref[...] += jnp.dot(m.astype(BF16), wo_ref[...], preferred_element_type=F32)

    @pl.when(k == nk - 1)
    def _():
        y = acc_ref[...]
        o_ref[...] = res_ref[...] + y * _rms_scale(y) * g_ref[...]


def _merge_out(out_a, out_b, out_c, w_a, w_b, w_c, proj2d, off, w_out, res, gain, l, tm, tk):
    m = out_a.shape[0]
    d = w_out.shape[-1]
    nk = d // tk
    ja, jb, jc = off["ga"] // tk, off["gb"] // tk, off["gc"] // tk
    return pl.pallas_call(
        functools.partial(_merge_out_body, nk=nk),
        grid=(m // tm, nk),
        in_specs=[pl.BlockSpec((tm, out_a.shape[1]), lambda i, k: (i, 0)),
                  pl.BlockSpec((tm, out_b.shape[1]), lambda i, k: (i, 0)),
                  pl.BlockSpec((tm, out_c.shape[1]), lambda i, k: (i, 0)),
                  _wspec(l, w_a.shape[1], tk, lambda i, k: (0, k)),
                  _wspec(l, w_b.shape[1], tk, lambda i, k: (0, k)),
                  _wspec(l, w_c.shape[1], tk, lambda i, k: (0, k)),
                  pl.BlockSpec((tm, tk), lambda i, k: (i, k + ja)),
                  pl.BlockSpec((tm, tk), lambda i, k: (i, k + jb)),
                  pl.BlockSpec((tm, tk), lambda i, k: (i, k + jc)),
                  _wspec(l, tk, d, lambda i, k: (k, 0)),
                  pl.BlockSpec((tm, d), lambda i, k: (i, 0)),
                  _wspec(l, 1, d, lambda i, k: (0, 0))],
        out_specs=pl.BlockSpec((tm, d), lambda i, k: (i, 0)),
        out_shape=jax.ShapeDtypeStruct((m, d), F32),
        scratch_shapes=[pltpu.VMEM((tm, d), F32)],
        compiler_params=_cparams(("parallel", "arbitrary")),
        name="merge_out",
    )(out_a, out_b, out_c, w_a, w_b, w_c, proj2d, proj2d, proj2d, w_out, res, gain)


def _unit_lower_inverse(lower, c):
    n = lower.shape[0]
    row = lax.broadcasted_iota(jnp.int32, (n, n), 0)
    col = lax.broadcasted_iota(jnp.int32, (n, n), 1)

    def same_block(shift):
        return (row >> shift) == (col >> shift)

    eye = jnp.where(row == col, 1.0, 0.0).astype(F32)
    d1 = jnp.where(same_block(3), lower, 0.0)
    d2 = _dot(d1, d1)
    d4 = _dot(d2, d2)
    x = eye - d1
    x = x + _dot(d2, x)
    x = x + _dot(d4, x)
    shift = 3
    while (1 << shift) < c:
        off_diag = jnp.where(jnp.logical_and(same_block(shift + 1), jnp.logical_not(same_block(shift))),
                             lower, 0.0)
        x = x - _dot(_dot(x, off_diag), x)
        shift += 1
    return x


def _unit_lower_inverses(lowers, c):
    n = lowers[0].shape[0]
    row = lax.broadcasted_iota(jnp.int32, (n, n), 0)
    col = lax.broadcasted_iota(jnp.int32, (n, n), 1)

    def same_block(shift):
        return (row >> shift) == (col >> shift)

    eye = jnp.where(row == col, 1.0, 0.0).astype(F32)
    blk = same_block(3)
    d1 = [jnp.where(blk, lo, 0.0) for lo in lowers]
    d2 = [_dot(d, d) for d in d1]
    d4 = [_dot(d, d) for d in d2]
    xs = [eye - d for d in d1]
    xs = [x + _dot(d, x) for d, x in zip(d2, xs)]
    xs = [x + _dot(d, x) for d, x in zip(d4, xs)]
    shift = 3
    while (1 << shift) < c:
        pick = jnp.logical_and(same_block(shift + 1), jnp.logical_not(same_block(shift)))
        offs = [jnp.where(pick, lo, 0.0) for lo in lowers]
        ts = [_dot(x, o) for x, o in zip(xs, offs)]
        xs = [x - _dot(t, x) for t, x in zip(ts, xs)]
        shift += 1
    return xs


def _gdn_prompt_body(alog_ref, dtb_ref, q_ref, k_ref, v_ref, z_ref, ba_ref, cwq_ref, cwk_ref, cwv_ref,
                     hq_ref, hk_ref, hv_ref, s0_ref, gain_ref, o_ref, sout_ref,
                     s_scr, ext_scr, *, tt, n_tiles, hb):
    c = tt
    h0 = pl.program_id(1) * hb
    it = pl.program_id(2)
    heads = range(hb)

    @pl.when(it == 0)
    def _():
        s_scr[...] = s0_ref[0]
        ext_scr[0, 0:SUBLANE, :] = hq_ref[0]
        ext_scr[1, 0:SUBLANE, :] = hk_ref[0]
        ext_scr[2, 0:SUBLANE, :] = hv_ref[0]

    ext_scr[0, SUBLANE:, :] = q_ref[0]
    ext_scr[1, SUBLANE:, :] = k_ref[0]
    ext_scr[2, SUBLANE:, :] = v_ref[0]

    def conv_act(i, cw_ref):
        base = SUBLANE - (CONV_K - 1)
        y = ext_scr[i, base:base + tt, :] * cw_ref[0:1, :]
        for j in range(1, CONV_K):
            y = y + ext_scr[i, base + j:base + j + tt, :] * cw_ref[j:j + 1, :]
        return _silu(y)

    q_all = conv_act(0, cwq_ref)
    k_all = conv_act(1, cwk_ref)
    v_all = conv_act(2, cwv_ref)

    if n_tiles > 1:
        for i in range(3):
            ext_scr[i, 0:SUBLANE, :] = ext_scr[i, tt:tt + SUBLANE, :]

    ba = ba_ref[0]
    lane = lax.broadcasted_iota(jnp.int32, (tt, LANE), 1)
    lane_c = lax.broadcasted_iota(jnp.int32, (c, LANE), 1)
    row = lax.broadcasted_iota(jnp.int32, (c, c), 0)
    col = lax.broadcasted_iota(jnp.int32, (c, c), 1)
    incl = row >= col
    strict = row > col
    eye = row == col
    tri = jnp.where(incl, 1.0, 0.0).astype(BF16)

    def hsl(hh):
        return slice(hh * LANE, (hh + 1) * LANE)

    qs, ks, vs, betas, g_terms = [], [], [], [], []
    for hh in heads:
        q = q_all[:, hsl(hh)]
        k = k_all[:, hsl(hh)]
        qs.append(q * lax.rsqrt(jnp.sum(q * q, axis=-1, keepdims=True) + L2_EPS) * (DK_A ** -0.5))
        ks.append(k * lax.rsqrt(jnp.sum(k * k, axis=-1, keepdims=True) + L2_EPS))
        vs.append(v_all[:, hsl(hh)])
        b_raw = jnp.sum(jnp.where(lane == h0 + hh, ba, 0.0), axis=1, keepdims=True)
        a_raw = jnp.sum(jnp.where(lane == h0 + hh + H_A, ba, 0.0), axis=1, keepdims=True)
        betas.append(jax.nn.sigmoid(b_raw))
        a_log = jnp.full((1, 1), alog_ref[h0 + hh], F32)
        xs = a_raw + dtb_ref[h0 + hh]
        softplus = jnp.maximum(xs, 0.0) + jnp.log1p(jnp.exp(-jnp.abs(xs)))
        g = -jnp.exp(a_log) * softplus
        g1, g2, g3 = _split3(jnp.broadcast_to(g, (c, LANE)))
        g_terms.append(jnp.where(lane_c == 0, g1, jnp.where(lane_c == 1, g2, g3)).astype(BF16))

    parts = [jnp.dot(tri, gt, preferred_element_type=F32) for gt in g_terms]
    kbs = [k * b for k, b in zip(ks, betas)]
    kqs = [_dot_nt(jnp.concatenate([kb, q], axis=0), k) for kb, q, k in zip(kbs, qs, ks)]
    gc_cols, decays = [], []
    for part in parts:
        gc_col = part[:, 0:1] + part[:, 1:2] + part[:, 2:3]
        gc_cb = jnp.broadcast_to(gc_col, (c, c))
        gc_rb = jnp.sum(jnp.where(eye, gc_cb, 0.0), axis=0, keepdims=True)
        decays.append(jnp.where(incl, jnp.exp(jnp.where(incl, gc_cb - gc_rb, 0.0)), 0.0))
        gc_cols.append(gc_col)
    lowers = [jnp.where(strict, kq[0:c] * dec, 0.0) for kq, dec in zip(kqs, decays)]
    a_intras = [jnp.where(incl, kq[c:2 * c] * dec, 0.0) for kq, dec in zip(kqs, decays)]
    tinvs = _unit_lower_inverses(lowers, c)
    egcs = [jnp.exp(gc) for gc in gc_cols]
    uws = [_dot(tinv, jnp.concatenate([v * b, kb * egc], axis=1))
           for tinv, v, b, kb, egc in zip(tinvs, vs, betas, kbs, egcs)]
    s_old = [s_scr[hh] for hh in heads]
    ws_qs = [_dot(jnp.concatenate([uw[:, DK_A:2 * DK_A], q * egc], axis=0), s)
             for uw, q, egc, s in zip(uws, qs, egcs, s_old)]
    v_news = [uw[:, 0:DK_A] - wq[0:c] for uw, wq in zip(uws, ws_qs)]
    outs = [wq[c:2 * c] + _dot(a, vn) for wq, a, vn in zip(ws_qs, a_intras, v_news)]
    for hh in heads:
        g_last = gc_cols[hh][c - 1:c, :]
        kd = ks[hh] * jnp.exp(g_last - gc_cols[hh])
        s_scr[hh] = s_old[hh] * jnp.exp(g_last) + _dot_tn(kd, v_news[hh])
    z_all = z_ref[0]
    for hh in heads:
        o = outs[hh]
        o = o * _rms_scale(o) * gain_ref[...]
        o = o * _silu(z_all[:, hsl(hh)])
        o_ref[0, :, hsl(hh)] = o.astype(o_ref.dtype)

    @pl.when(it == n_tiles - 1)
    def _():
        sout_ref[0] = s_scr[...]


def _gdn(proj, off, conv_w, conv_hist8, s0, a_log, dt_bias, gdn_gain, l, *, tile, hb):
    b, t, _ = proj.shape
    assert t % tile == 0
    tt = tr = tile
    n_tiles = t // tile
    wid = hb * LANE
    cq, ck, cv, cz = (off[n] // wid for n in ("qa", "ka", "va", "za"))
    cba = off["ba"] // LANE
    nhb = H_A // hb

    def tok(col0):
        return pl.BlockSpec((1, tr, wid), lambda bi, hi, ti: (bi, ti, col0 + hi))

    def per_head(col0, rows):
        return _wspec(l, rows, wid, lambda bi, hi, ti: (0, col0 + hi))

    def hist(col0):
        return pl.BlockSpec((1, SUBLANE, wid), lambda bi, hi, ti: (bi, 0, col0 + hi))

    smem = pl.BlockSpec(memory_space=pltpu.SMEM)
    state_spec = pl.BlockSpec((1, hb, DK_A, DK_A), lambda bi, hi, ti: (bi, hi, 0, 0))
    body = functools.partial(_gdn_prompt_body, tt=tt, n_tiles=n_tiles, hb=hb)
    return pl.pallas_call(
        body,
        grid=(b, nhb, n_tiles),
        in_specs=[smem, smem,
                  tok(cq), tok(ck), tok(cv), tok(cz),
                  pl.BlockSpec((1, tr, LANE), lambda bi, hi, ti: (bi, ti, cba)),
                  per_head(0, CONV_K), per_head(nhb, CONV_K), per_head(2 * nhb, CONV_K),
                  hist(0), hist(nhb), hist(2 * nhb),
                  state_spec,
                  _wspec(l, 1, DK_A, lambda bi, hi, ti: (0, 0))],
        out_specs=[pl.BlockSpec((1, tr, wid), lambda bi, hi, ti: (bi, ti, hi)),
                   state_spec],
        out_shape=[jax.ShapeDtypeStruct((b, t, QK_A), BF16),
                   jax.ShapeDtypeStruct((b, H_A, DK_A, DK_A), F32)],
        scratch_shapes=[pltpu.VMEM((hb, DK_A, DK_A), F32),
                        pltpu.VMEM((3, tt + SUBLANE, wid), F32)],
        compiler_params=_cparams(("parallel", "parallel", "arbitrary")),
        name="gdn",
    )(a_log[l], dt_bias[l], proj, proj, proj, proj, proj, conv_w, conv_w, conv_w,
      conv_hist8, conv_hist8, conv_hist8, s0, gdn_gain)


def _gdn_sample_body(alog_ref, dtb_ref, q_ref, k_ref, v_ref, z_ref, ba_ref, cwq_ref, cwk_ref, cwv_ref,
                     hq_ref, hk_ref, hv_ref, s0_ref, gain_ref, o_ref, sout_ref, ext_scr, z_scr, ba_scr, *, tr):
    tp = SUBLANE
    n = H_A * tp
    for i, (x_ref, h_ref) in enumerate(((q_ref, hq_ref), (k_ref, hk_ref), (v_ref, hv_ref))):
        ext_scr[i, 0:SUBLANE, :] = h_ref[0]
        ext_scr[i, SUBLANE:, :] = jnp.zeros((tp, QK_A), F32)
        ext_scr[i, SUBLANE:SUBLANE + tr, :] = x_ref[0]

    def conv_act(i, cw_ref):
        base = SUBLANE - (CONV_K - 1)
        y = ext_scr[i, base:base + tp, :] * cw_ref[0:1, :]
        for j in range(1, CONV_K):
            y = y + ext_scr[i, base + j:base + j + tp, :] * cw_ref[j:j + 1, :]
        return _silu(y)

    def pack(x):
        return jnp.concatenate([x[:, h * LANE:(h + 1) * LANE] for h in range(H_A)], axis=0)

    q = pack(conv_act(0, cwq_ref))
    k = pack(conv_act(1, cwk_ref))
    v = pack(conv_act(2, cwv_ref))
    q = q * lax.rsqrt(jnp.sum(q * q, axis=-1, keepdims=True) + L2_EPS) * (DK_A ** -0.5)
    k = k * lax.rsqrt(jnp.sum(k * k, axis=-1, keepdims=True) + L2_EPS)

    ba_scr[...] = jnp.zeros((tp, LANE), F32)
    ba_scr[0:tr, :] = ba_ref[0]
    ba = ba_scr[...]
    b_raw = jnp.concatenate([ba[:, h:h + 1] for h in range(H_A)], axis=0)
    a_raw = jnp.concatenate([ba[:, H_A + h:H_A + h + 1] for h in range(H_A)], axis=0)
    a_log = jnp.concatenate([jnp.full((tp, 1), alog_ref[h], F32) for h in range(H_A)], axis=0)
    dtb = jnp.concatenate([jnp.full((tp, 1), dtb_ref[h], F32) for h in range(H_A)], axis=0)
    beta = jax.nn.sigmoid(b_raw)
    xs = a_raw + dtb
    g = -jnp.exp(a_log) * (jnp.maximum(xs, 0.0) + jnp.log1p(jnp.exp(-jnp.abs(xs))))

    valid = (lax.broadcasted_iota(jnp.int32, (n, 1), 0) & (tp - 1)) < tr
    q = jnp.where(valid, q, 0.0)
    k = jnp.where(valid, k, 0.0)
    v = jnp.where(valid, v, 0.0)
    beta = jnp.where(valid, beta, 0.0)
    g = jnp.where(valid, g, 0.0)

    row = lax.broadcasted_iota(jnp.int32, (n, n), 0)
    col = lax.broadcasted_iota(jnp.int32, (n, n), 1)
    same_head = (row >> 3) == (col >> 3)
    incl = jnp.logical_and(same_head, row >= col)
    strict = jnp.logical_and(same_head, row > col)
    eye = row == col
    sum_mat = jnp.concatenate([jnp.where(incl, 1.0, 0.0), jnp.where(same_head, 1.0, 0.0)], axis=0).astype(BF16)
    g1, g2, g3 = _split3(jnp.broadcast_to(g, (n, LANE)))
    lane_c = lax.broadcasted_iota(jnp.int32, (n, LANE), 1)
    g_terms = jnp.where(lane_c == 0, g1, jnp.where(lane_c == 1, g2, g3))
    part = jnp.dot(sum_mat, g_terms.astype(BF16), preferred_element_type=F32)
    sums = part[:, 0:1] + part[:, 1:2] + part[:, 2:3]
    gc_col = sums[0:n]
    g_last = sums[n:2 * n]
    gc_cb = jnp.broadcast_to(gc_col, (n, n))
    gc_rb = jnp.sum(jnp.where(eye, gc_cb, 0.0), axis=0, keepdims=True)
    decay = jnp.where(incl, jnp.exp(jnp.where(incl, gc_cb - gc_rb, 0.0)), 0.0)
    kb = k * beta
    kq = _dot_nt(jnp.concatenate([kb, q], axis=0), k)
    lower = jnp.where(strict, kq[0:n] * decay, 0.0)
    a_intra = jnp.where(incl, kq[n:2 * n] * decay, 0.0)
    tinv = _unit_lower_inverse(lower, tp)
    egc = jnp.exp(gc_col)
    uw = _dot(tinv, jnp.concatenate([v * beta, kb * egc], axis=1))
    kd = k * jnp.exp(g_last - gc_col)

    row_e = lax.broadcasted_iota(jnp.int32, (n, QK_A), 0)
    col_e = lax.broadcasted_iota(jnp.int32, (n, QK_A), 1)
    own = (row_e >> 3) == (col_e >> 7)

    def expand(x):
        return jnp.where(own, jnp.concatenate([x] * H_A, axis=1), 0.0)

    s = s0_ref[0].reshape(H_A * DK_A, DK_A)
    ws_qs = _dot(jnp.concatenate([expand(uw[:, DK_A:2 * DK_A]), expand(q * egc)], axis=0), s)
    v_new = uw[:, 0:DK_A] - ws_qs[0:n]
    o = ws_qs[n:2 * n] + _dot(a_intra, v_new)
    upd = _dot_tn(expand(kd), v_new)
    for h in range(H_A):
        eg = jnp.exp(g_last[h * tp:h * tp + 1, :])
        sout_ref[0, h] = s[h * DK_A:(h + 1) * DK_A] * eg + upd[h * DK_A:(h + 1) * DK_A]

    o = o * _rms_scale(o) * gain_ref[...]
    z_scr[...] = jnp.zeros((tp, QK_A), F32)
    z_scr[0:tr, :] = z_ref[0]
    o = o * _silu(pack(z_scr[...]))
    o_tok = jnp.concatenate([o[h * tp:(h + 1) * tp] for h in range(H_A)], axis=1)
    o_ref[0] = o_tok[0:tr].astype(o_ref.dtype)


def _gdn_sample(proj, off, conv_w, conv_hist8, s0, a_log, dt_bias, gdn_gain, l):
    b, t, _ = proj.shape
    assert t <= SUBLANE
    cq, ck, cv, cz = (off[n] // QK_A for n in ("qa", "ka", "va", "za"))
    cba = off["ba"] // LANE

    def tok(col0):
        return pl.BlockSpec((1, t, QK_A), lambda bi: (bi, 0, col0))

    def conv(col0):
        return _wspec(l, CONV_K, QK_A, lambda bi: (0, col0))

    def hist(col0):
        return pl.BlockSpec((1, SUBLANE, QK_A), lambda bi: (bi, 0, col0))

    smem = pl.BlockSpec(memory_space=pltpu.SMEM)
    state_spec = pl.BlockSpec((1, H_A, DK_A, DK_A), lambda bi: (bi, 0, 0, 0))
    return pl.pallas_call(
        functools.partial(_gdn_sample_body, tr=t),
        grid=(b,),
        in_specs=[smem, smem, tok(cq), tok(ck), tok(cv), tok(cz),
                  pl.BlockSpec((1, t, LANE), lambda bi: (bi, 0, cba)),
                  conv(0), conv(1), conv(2), hist(0), hist(1), hist(2),
                  state_spec, _wspec(l, 1, DK_A, lambda bi: (0, 0))],
        out_specs=[pl.BlockSpec((1, t, QK_A), lambda bi: (bi, 0, 0)), state_spec],
        out_shape=[jax.ShapeDtypeStruct((b, t, QK_A), BF16),
                   jax.ShapeDtypeStruct((b, H_A, DK_A, DK_A), F32)],
        scratch_shapes=[pltpu.VMEM((3, 2 * SUBLANE, QK_A), F32),
                        pltpu.VMEM((SUBLANE, QK_A), F32),
                        pltpu.VMEM((SUBLANE, LANE), F32)],
        compiler_params=_cparams(("parallel",)),
        name="gdn_sample",
    )(a_log[l], dt_bias[l], proj, proj, proj, proj, proj, conv_w, conv_w, conv_w,
      conv_hist8, conv_hist8, conv_hist8, s0, gdn_gain)


def _attn_body(q0_ref, q1_ref, q2_ref, k0_ref, k1_ref, k2_ref, v0_ref, v1_ref, v2_ref, o_ref,
               kb_scr, vb_scr, d0_scr, d1_scr, d2_scr, *, s_len):
    q_refs = (q0_ref, q1_ref, q2_ref)
    k_refs = (k0_ref, k1_ref, k2_ref)
    v_refs = (v0_ref, v1_ref, v2_ref)
    dist_scrs = (d0_scr, d1_scr, d2_scr)
    tq_n = Q_TILE
    scale = HD_B ** -0.5
    for gi, (win, dil) in enumerate(DIL_GROUPS):
        kb_scr[gi] = k_refs[gi][0].astype(BF16)
        vb_scr[gi] = v_refs[gi][0].astype(BF16)
        nk = dist_scrs[gi].shape[1]
        rc = lax.broadcasted_iota(jnp.int32, (tq_n, nk), 0) - lax.broadcasted_iota(jnp.int32, (tq_n, nk), 1)
        dist_scrs[gi][...] = jnp.where((rc & (dil - 1)) == 0, rc, WRONG_RESIDUE).astype(F32)

    def tile(i, carry, cap):
        q0 = pl.multiple_of(i * tq_n, tq_n)
        scores = []
        starts = []
        m = None
        for gi, (win, dil) in enumerate(DIL_GROUPS):
            nk = min(cap, win + tq_n)
            k0 = pl.multiple_of(jnp.clip(q0 - win, 0, cap - nk), tq_n)
            q = q_refs[gi][0, pl.ds(q0, tq_n), :]
            s = _dot_nt(q, kb_scr[gi, pl.ds(k0, nk), :]) * scale
            d = dist_scrs[gi][:, 0:nk] + (q0 - k0).astype(F32)
            ok = d >= 0.0
            if win < s_len - 1:
                ok = jnp.logical_and(ok, d <= float(win))
            s = jnp.where(ok, s, -jnp.inf)
            mg = jnp.max(s, axis=1, keepdims=True)
            m = mg if m is None else jnp.maximum(m, mg)
            scores.append(s)
            starts.append((k0, nk))
        acc = jnp.zeros((tq_n, HD_B), F32)
        den = jnp.zeros((tq_n, 1), F32)
        for gi in range(N_GROUPS):
            k0, nk = starts[gi]
            p = jnp.exp(scores[gi] - m)
            den = den + jnp.sum(p, axis=1, keepdims=True)
            acc = acc + _dot(p, vb_scr[gi, pl.ds(k0, nk), :])
        o_ref[0, pl.ds(q0, tq_n), :] = (acc / den).astype(o_ref.dtype)
        return carry

    n_tiles = s_len // tq_n
    seg = max(1, n_tiles // ATTN_SEGMENTS)
    for lo in range(0, n_tiles, seg):
        hi = min(lo + seg, n_tiles)
        lax.fori_loop(lo, hi, functools.partial(tile, cap=hi * tq_n), 0)


def _attn_prompt(proj, off):
    b, s_len, _ = proj.shape
    cq, ck, cv = (off[n] // LANE for n in ("qb", "kb", "vb"))

    def spec(col0, gi):
        return pl.BlockSpec((1, s_len, LANE), lambda bi, hi: (bi, 0, col0 + gi * H_G + hi))

    return pl.pallas_call(
        functools.partial(_attn_body, s_len=s_len),
        grid=(b, H_G),
        in_specs=[spec(c0, gi) for c0 in (cq, ck, cv) for gi in range(N_GROUPS)],
        out_specs=pl.BlockSpec((1, s_len, LANE), lambda bi, hi: (bi, 0, hi)),
        out_shape=jax.ShapeDtypeStruct((b, s_len, OUT_B), BF16),
        scratch_shapes=[pltpu.VMEM((N_GROUPS, s_len, HD_B), BF16),
                        pltpu.VMEM((N_GROUPS, s_len, HD_B), BF16)]
        + [pltpu.VMEM((Q_TILE, min(s_len, win + Q_TILE)), F32) for (win, _) in DIL_GROUPS],
        compiler_params=_cparams(("parallel", "parallel")),
        name="attn_prompt",
    )(*([proj] * 9))


SHIFT_UNITS = 128


def _shift_body(a_ref, b_ref, o_ref):
    r = a_ref.shape[0]
    o_ref[0:r - 1] = a_ref[1:r]
    o_ref[r - 1:r] = b_ref[...]


def _shift_cache(c, t_new):
    depth, batch, win = c.shape[:3]
    nu = win // t_new
    r = min(SHIFT_UNITS, nu)
    assert win % t_new == 0 and nu % r == 0
    view = c.reshape(depth, batch, nu, t_new, 2, H_G, HD_B)
    tail = (t_new, 2, H_G, HD_B)
    zeros = (0,) * len(tail)
    out = pl.pallas_call(
        _shift_body,
        grid=(depth, batch, nu // r),
        in_specs=[pl.BlockSpec((None, None, r) + tail, lambda l, b, j: (l, b, j) + zeros),
                  pl.BlockSpec((None, None, 1) + tail,
                               lambda l, b, j: (l, b, jnp.minimum((j + 1) * r, nu - 1)) + zeros)],
        out_specs=pl.BlockSpec((None, None, r) + tail, lambda l, b, j: (l, b, j) + zeros),
        out_shape=jax.ShapeDtypeStruct(view.shape, view.dtype),
        compiler_params=_cparams(("parallel", "parallel", "parallel")),
        name="cache_shift",
    )(view, view)
    return out.reshape(c.shape)


def _attn_sample_body(q_ref, kn_ref, vn_ref, c1_ref, c2_ref, c3_ref, w1_in, w2_in, w3_in,
                      o_ref, w1_ref, w2_ref, w3_ref, *, t_new):
    del w1_in, w2_in, w3_in
    scale = HD_B ** -0.5
    q = q_ref[0]
    kn = kn_ref[0]
    vn = vn_ref[0]
    caches = (c1_ref, c2_ref, c3_ref)
    wins = (w1_ref, w2_ref, w3_ref)
    row_c = lax.broadcasted_iota(jnp.int32, (N_BACK, H_G, 1), 0)

    def heads(x, t, gi):
        c0 = gi * OUT_B
        return jnp.concatenate([x[t:t + 1, c0 + h * HD_B:c0 + (h + 1) * HD_B] for h in range(H_G)], axis=0)

    k_new = [[heads(kn, t, gi) for t in range(t_new)] for gi in range(N_GROUPS)]
    v_new = [[heads(vn, t, gi) for t in range(t_new)] for gi in range(N_GROUPS)]
    for gi in range(N_GROUPS):
        for t in range(t_new):
            wins[gi][0, t, 0] = k_new[gi][t]
            wins[gi][0, t, 1] = v_new[gi][t]

    for t in range(t_new):
        parts = []
        for gi in range(N_GROUPS):
            qv = heads(q, t, gi)
            r = 0 if gi == 0 else t
            kc = caches[gi][0, :, r, 0]
            vc = caches[gi][0, :, r, 1]
            s_c = jnp.sum(kc * qv[None], axis=-1, keepdims=True) * scale
            if gi == 0:
                s_c = jnp.where(row_c >= t, s_c, -jnp.inf)
                news = list(range(t + 1))
            else:
                news = [t]
            s_n = [jnp.sum(k_new[gi][tn] * qv, axis=-1, keepdims=True) * scale for tn in news]
            m = jnp.max(s_c, axis=0)
            for sn in s_n:
                m = jnp.maximum(m, sn)
            p_c = jnp.exp(s_c - m[None])
            l = jnp.sum(p_c, axis=0)
            o = jnp.sum(p_c * vc, axis=0)
            for tn, sn in zip(news, s_n):
                p_n = jnp.exp(sn - m)
                l = l + p_n
                o = o + p_n * v_new[gi][tn]
            parts.append((o, m, l))
        m_all = jnp.maximum(jnp.maximum(parts[0][1], parts[1][1]), parts[2][1])
        num = 0.0
        den = 0.0
        for (o, m, l) in parts:
            e = jnp.exp(m - m_all)
            num = num + e * o
            den = den + e * l
        out = (num / den).astype(o_ref.dtype)
        for h in range(H_G):
            o_ref[0, t:t + 1, h * HD_B:(h + 1) * HD_B] = out[h:h + 1, :]


def _attn_sample(proj, off, caches, windows, l):
    b, t, _ = proj.shape
    depth = caches[0].shape[0]
    assert t <= DIL_GROUPS[1][1] and DIL_GROUPS[0][1] == 1
    views, cspecs = [], []
    for c, (win, dil) in zip(caches, DIL_GROUPS):
        views.append(c.reshape(depth, b, N_BACK, dil, 2, H_G, HD_B))
        rows = min(dil, t)
        cspecs.append(pl.BlockSpec((None, 1, N_BACK, rows, 2, H_G, HD_B),
                                   lambda bi: (l, bi, 0, 0, 0, 0, 0)))
    wspecs = [pl.BlockSpec((None, 1, t, 2, H_G, HD_B),
                           functools.partial(lambda bi, blk: (l, bi, blk, 0, 0, 0), blk=win // t - 1))
              for (win, _) in DIL_GROUPS]

    def pspec(name):
        return pl.BlockSpec((1, t, QKV_B), lambda bi: (bi, 0, off[name] // QKV_B))

    any_spec = pl.BlockSpec(memory_space=pl.ANY)
    res = pl.pallas_call(
        functools.partial(_attn_sample_body, t_new=t),
        grid=(b,),
        in_specs=[pspec("qb"), pspec("kb"), pspec("vb")] + cspecs + [any_spec] * 3,
        out_specs=[pl.BlockSpec((1, t, OUT_B), lambda bi: (bi, 0, 0))] + wspecs,
        out_shape=[jax.ShapeDtypeStruct((b, t, OUT_B), BF16)]
        + [jax.ShapeDtypeStruct(w.shape, w.dtype) for w in windows],
        input_output_aliases={6: 1, 7: 2, 8: 3},
        compiler_params=_cparams(("parallel",)),
        name="attn_sample",
    )(proj, proj, proj, *views, *windows)
    return res[0], tuple(res[1:])


def _pool_body(u_ref, hist_ref, wp_ref, ps_ref, o_ref, ext_scr, *, tt, tr, pos0, n_tiles):
    hp = POOL_HIST + 1
    it = pl.program_id(1)

    @pl.when(it == 0)
    def _():
        ext_scr[0:hp, :] = hist_ref[0]

    if tr < tt:
        ext_scr[hp:, :] = jnp.zeros((tt, C_POOL), F32)
    ext_scr[hp:hp + tr, :] = u_ref[0]

    pos = pos0 + it * tt + lax.broadcasted_iota(jnp.int32, (tt, 1), 0)
    outs = []
    for gi, win in enumerate(POOL_WINDOWS):
        lo, hi = gi * CG, (gi + 1) * CG
        x = ext_scr[hp:hp + tt, lo:hi]
        acc = x
        for back in range(1, win):
            acc = acc + ext_scr[hp - back:hp - back + tt, lo:hi]
        cnt = jnp.minimum(win, pos + 1).astype(F32)
        pooled = acc / cnt - x
        outs.append(jnp.dot(pooled.astype(BF16), wp_ref[gi], preferred_element_type=F32))
    oc = jnp.concatenate(outs, axis=1) * ps_ref[...]
    o_ref[0] = oc[0:tr].astype(o_ref.dtype)

    if n_tiles > 1:
        ext_scr[0:hp, :] = ext_scr[tt:tt + hp, :]


def _pool(proj, off, hist16, w_pool, pool_scale, l, pos0, tile):
    b, t, _ = proj.shape
    hp = POOL_HIST + 1
    if t >= tile:
        tt = tr = tile
        n_tiles = t // tile
    else:
        tt, tr, n_tiles = SUBLANE, t, 1
    cu = off["uc"] // C_POOL
    assert off["uc"] % C_POOL == 0
    ng = len(POOL_WINDOWS)
    return pl.pallas_call(
        functools.partial(_pool_body, tt=tt, tr=tr, pos0=pos0, n_tiles=n_tiles),
        grid=(b, n_tiles),
        in_specs=[pl.BlockSpec((1, tr, C_POOL), lambda bi, ti: (bi, ti, cu)),
                  pl.BlockSpec((1, hp, C_POOL), lambda bi, ti: (bi, 0, 0)),
                  pl.BlockSpec((None, ng, CG, CG), lambda bi, ti: (l, 0, 0, 0)),
                  _wspec(l, 1, C_POOL, lambda bi, ti: (0, 0))],
        out_specs=pl.BlockSpec((1, tr, C_POOL), lambda bi, ti: (bi, ti, 0)),
        out_shape=jax.ShapeDtypeStruct((b, t, C_POOL), BF16),
        scratch_shapes=[pltpu.VMEM((hp + tt, C_POOL), F32)],
        compiler_params=_cparams(("parallel", "arbitrary")),
        name="pool",
    )(proj, hist16, w_pool, pool_scale)


def _tile_rows(m, pref):
    return pref if m % pref == 0 else m


def _layer(x, wts, l, conv_hist, s0, pool_hist, caches, windows, prompt):
    (w_in, conv_w, a_log, dt_bias, gdn_gain, w_pool, pool_scale, w_a, w_b, w_c, w_out, w_gu, w_down,
     g_pre_mix, g_post_mix, g_pre_ffn, g_post_ffn) = wts
    b, t, d = x.shape
    m = b * t
    off, npad = _proj_layout(d)
    x2 = x.reshape(m, d)
    tm = _tile_rows(m, TM_NORM)
    tm_wide = _tile_rows(m, TM_WIDE)

    proj2 = _norm_matmul(x2, g_pre_mix, w_in, l, tm_wide, TN)
    proj = proj2.reshape(b, t, npad)

    hist8 = jnp.concatenate([jnp.zeros((b, SUBLANE - (CONV_K - 1), 3 * QK_A), F32), conv_hist], axis=1)
    if prompt:
        out_a, s_new = _gdn(proj, off, conv_w, hist8, s0, a_log, dt_bias, gdn_gain, l, tile=GDN_TILE, hb=4)
    else:
        out_a, s_new = _gdn_sample(proj, off, conv_w, hist8, s0, a_log, dt_bias, gdn_gain, l)
    qa0 = off["qa"]
    conv_new = jnp.concatenate([conv_hist, proj[:, :, qa0:qa0 + 3 * QK_A]], axis=1)[:, -(CONV_K - 1):]

    kb0, vb0 = off["kb"], off["vb"]
    if prompt:
        out_b = _attn_prompt(proj, off).reshape(m, OUT_B)
        win_new = []
        for gi, (win, _) in enumerate(DIL_GROUPS):
            rows = min(win, t)
            kg = proj[:, t - rows:, kb0 + gi * OUT_B:kb0 + (gi + 1) * OUT_B].reshape(b, rows, 1, H_G, HD_B)
            vg = proj[:, t - rows:, vb0 + gi * OUT_B:vb0 + (gi + 1) * OUT_B].reshape(b, rows, 1, H_G, HD_B)
            win_new.append(jnp.concatenate([kg, vg], axis=2))
    else:
        out_b, win_new = _attn_sample(proj, off, caches, windows, l)
        out_b = out_b.reshape(m, OUT_B)

    hist16 = jnp.concatenate([jnp.zeros((b, 1, C_POOL), F32), pool_hist], axis=1)
    out_c = _pool(proj, off, hist16, w_pool, pool_scale, l, 0 if prompt else SAMPLE_POS0, POOL_TILE)
    uc0 = off["uc"]
    pool_new = jnp.concatenate([pool_hist, proj[:, :, uc0:uc0 + C_POOL]], axis=1)[:, -POOL_HIST:]

    x2 = _merge_out(out_a.reshape(m, QK_A), out_b, out_c.reshape(m, C_POOL), w_a, w_b, w_c, proj2, off,
                    w_out, x2, g_post_mix, l, tm, TN)
    act = _ffn_up(x2, g_pre_ffn, w_gu, l, tm_wide, TN)
    d_ff = act.shape[1]
    x2 = _mm_norm_res(act, w_down, x2, g_post_ffn, l, tm, d_ff // 4 if d_ff % (4 * LANE) == 0 else 512)
    return x2.reshape(b, t, d), (win_new[0], win_new[1], win_new[2], s_new, conv_new, pool_new)


def kernel(x_prompt, x_sample, cache_win1, cache_win2, cache_win3, state_gdn, state_conv, state_pool,
           w_in, conv_w, a_log, dt_bias, gdn_gain, w_pool, pool_scale, w_br_a, w_br_b, w_br_c,
           w_out, w_gu, w_down, g_pre_mix, g_post_mix, g_pre_ffn, g_post_ffn):
    depth = w_in.shape[0]
    bp = x_prompt.shape[0]
    d = x_prompt.shape[-1]

    def row(v):
        return v.reshape(depth, 1, v.shape[-1])

    wts = (jnp.swapaxes(w_in, 1, 2).astype(BF16), conv_w, a_log, dt_bias, row(gdn_gain),
           w_pool.astype(BF16), row(pool_scale),
           w_br_a.astype(BF16), w_br_b.astype(BF16), w_br_c.astype(BF16),
           w_out.astype(BF16), w_gu.astype(BF16), w_down.astype(BF16),
           row(g_pre_mix), row(g_post_mix), row(g_pre_ffn), row(g_post_ffn))
    yp, ys = x_prompt, x_sample
    caches = (cache_win1, cache_win2, cache_win3)
    windows = tuple(_shift_cache(c, x_sample.shape[1]) for c in caches)
    new_p, new_s = [], []
    for l in range(depth):
        yp, st_p = _layer(yp, wts, l,
                          jnp.zeros((bp, CONV_K - 1, 3 * QK_A), F32),
                          jnp.zeros((bp, H_A, DK_A, DK_A), F32),
                          jnp.zeros((bp, POOL_HIST, C_POOL), F32),
                          None, None, True)
        ys, st_s = _layer(ys, wts, l, state_conv[l], state_gdn[l], state_pool[l],
                          caches, windows, False)
        windows = st_s[:3]
        new_p.append(st_p)
        new_s.append(st_s)
    outs_p = [jnp.stack([st[i] for st in new_p], axis=0) for i in range(6)]
    outs_s = [jnp.stack([st[i] for st in new_s], axis=0) for i in range(3, 6)]
    return (yp, ys, *outs_p, *windows, *outs_s)
```
